```python
import jax
import jax.numpy as jnp
from jax import lax
import numpy as np

D_MODEL = 4096
BATCH = 1
SEQ = 8192
DEPTH = 4
DEC_BATCH = 4
DEC_SEQ = 4096
PAST_LEN = 128

HEAD_DIM = 128
N_BRANCH = 4
BRANCH_WIDTH = D_MODEL // N_BRANCH
A_HEADS = BRANCH_WIDTH // HEAD_DIM
A_KV_HEADS = A_HEADS // 4
WINDOW = 128
BLOCK = 128
B_HEADS = BRANCH_WIDTH // HEAD_DIM
B_KV_HEADS = B_HEADS // 4
GRID_W = 64
ROPE_THETA = 10000.0
ROPE_FREQS = HEAD_DIM // 4
C_HEADS = BRANCH_WIDTH // HEAD_DIM
C_WIDTH = C_HEADS * HEAD_DIM
CHUNK = 128
LRU_WIDTH = BRANCH_WIDTH
LRU_BLOCKS = BRANCH_WIDTH // HEAD_DIM
LRU_BLOCK_W = LRU_WIDTH // LRU_BLOCKS
LRU_C = 8.0
CONV_W = 4
CONV_LEFT = 2
N_META = 16
D_FF = 4 * D_MODEL
NORM_EPS = 1e-6
GN_EPS = 1e-5
NEG_INF = -1e30
IN_SPLITS = (A_HEADS * HEAD_DIM, A_KV_HEADS * HEAD_DIM, A_KV_HEADS * HEAD_DIM,
             B_HEADS * HEAD_DIM, B_KV_HEADS * HEAD_DIM, B_KV_HEADS * HEAD_DIM,
             C_WIDTH, C_WIDTH, C_WIDTH, C_WIDTH,
             LRU_WIDTH, LRU_WIDTH,
             N_BRANCH * D_MODEL)
IN_WIDTH = (A_HEADS + 2 * A_KV_HEADS + B_HEADS + 2 * B_KV_HEADS) * HEAD_DIM + 4 * C_WIDTH + 2 * LRU_WIDTH + N_BRANCH * D_MODEL

kernel_name = 'hybrid_bidir_meta_encoder'


def rms_norm(x, w):
    xf = x.astype(jnp.float32)
    y = xf * lax.rsqrt(jnp.mean(xf * xf, axis=-1, keepdims=True) + NORM_EPS)
    return (y * w.astype(jnp.float32)).astype(x.dtype)


def alibi_slopes(n_heads):
    return jnp.exp2(-8.0 * jnp.arange(1, n_heads + 1, dtype=jnp.float32) / n_heads)


def softmax_with_sink(logits, sink):
    sink_col = jnp.broadcast_to(sink[:, :, None, None], logits.shape[:-1] + (1,))
    p = jax.nn.softmax(jnp.concatenate([logits, sink_col], axis=-1), axis=-1)
    return p[..., :-1]


def windowed_sink_attention(q, k, v, sink):
    bsz, L = q.shape[0], q.shape[1]
    n = L - N_META
    nb = n // BLOCK
    g = A_HEADS // A_KV_HEADS
    scale = HEAD_DIM ** -0.5
    slopes = alibi_slopes(A_HEADS).reshape(A_KV_HEADS, g)
    sink = sink.astype(jnp.float32).reshape(A_KV_HEADS, g)
    qg = q.reshape(bsz, L, A_KV_HEADS, g, HEAD_DIM)
    q_meta = qg[:, :N_META]
    q_real = qg[:, N_META:].reshape(bsz, nb, BLOCK, A_KV_HEADS, g, HEAD_DIM)
    k_meta, k_real = k[:, :N_META], k[:, N_META:]
    v_meta, v_real = v[:, :N_META], v[:, N_META:]

    def band(t):
        tb = t.reshape(bsz, nb, BLOCK, A_KV_HEADS, HEAD_DIM)
        tp = jnp.pad(tb, ((0, 0), (1, 1), (0, 0), (0, 0), (0, 0)))
        return jnp.concatenate([tp[:, :-2], tp[:, 1:-1], tp[:, 2:]], axis=2)

    k_band, v_band = band(k_real), band(v_real)
    i = jnp.arange(BLOCK)[:, None]
    j = jnp.arange(3 * BLOCK)[None, :]
    dist = jnp.abs(i + BLOCK - j)
    src_blk = jnp.arange(nb)[:, None] - 1 + jnp.arange(3 * BLOCK)[None, :] // BLOCK
    valid = (dist <= WINDOW)[None] & ((src_blk >= 0) & (src_blk < nb))[:, None, :]
    s_band = jnp.einsum('bcikgd,bcjkd->bckgij', q_real, k_band, preferred_element_type=jnp.float32) * scale
    s_band = s_band - slopes[:, :, None, None] * dist.astype(jnp.float32)
    s_band = jnp.where(valid[None, :, None, None], s_band, NEG_INF)
    s_mk = jnp.einsum('bcikgd,bjkd->bckgij', q_real, k_meta, preferred_element_type=jnp.float32) * scale
    p = softmax_with_sink(jnp.concatenate([s_mk, s_band], axis=-1), sink).astype(v.dtype)
    out_real = (jnp.einsum('bckgij,bjkd->bcikgd', p[..., :N_META], v_meta)
                + jnp.einsum('bckgij,bcjkd->bcikgd', p[..., N_META:], v_band))
    out_real = out_real.reshape(bsz, n, A_HEADS * HEAD_DIM)

    im = jnp.arange(N_META)[:, None]
    jr = jnp.arange(BLOCK)[None, :]
    dist_m = (N_META + jr - im).astype(jnp.float32)
    s_mm = jnp.einsum('bikgd,bjkd->bkgij', q_meta, k_meta, preferred_element_type=jnp.float32) * scale
    s_mr = jnp.einsum('bikgd,bjkd->bkgij', q_meta, k_real[:, :BLOCK], preferred_element_type=jnp.float32) * scale
    s_mr = jnp.where(dist_m <= WINDOW, s_mr - slopes[:, :, None, None] * dist_m, NEG_INF)
    pm = softmax_with_sink(jnp.concatenate([s_mm, s_mr], axis=-1), sink).astype(v.dtype)
    out_meta = (jnp.einsum('bkgij,bjkd->bikgd', pm[..., :N_META], v_meta)
                + jnp.einsum('bkgij,bjkd->bikgd', pm[..., N_META:], v_real[:, :BLOCK]))
    out_meta = out_meta.reshape(bsz, N_META, A_HEADS * HEAD_DIM)
    return jnp.concatenate([out_meta, out_real], axis=1)


def head_rms_norm(x, w):
    xf = x.astype(jnp.float32)
    y = xf * lax.rsqrt(jnp.mean(xf * xf, axis=-1, keepdims=True) + NORM_EPS)
    return (y * w.astype(jnp.float32)).astype(x.dtype)


def axial_rope_angles(n):
    rows = n // GRID_W
    r, c = jnp.meshgrid(jnp.arange(rows), jnp.arange(GRID_W), indexing='ij')
    row = jnp.concatenate([jnp.full((N_META,), -1, dtype=jnp.int32), r.reshape(-1).astype(jnp.int32)])
    col = jnp.concatenate([jnp.arange(N_META, dtype=jnp.int32), c.reshape(-1).astype(jnp.int32)])
    pos = jnp.stack([row, col], axis=-1).astype(jnp.float32)
    inv_freq = ROPE_THETA ** (-jnp.arange(ROPE_FREQS, dtype=jnp.float32) / ROPE_FREQS)
    ang = pos[:, :, None] * inv_freq
    return jnp.cos(ang), jnp.sin(ang)


def apply_axial_rope(x, cos, sin):
    bsz, L, h, _ = x.shape
    xr = x.astype(jnp.float32).reshape(bsz, L, h, 2, 2, ROPE_FREQS)
    x1, x2 = xr[..., 0, :], xr[..., 1, :]
    c = cos[None, :, None]
    s = sin[None, :, None]
    out = jnp.stack([x1 * c - x2 * s, x2 * c + x1 * s], axis=-2)
    return out.reshape(bsz, L, h, HEAD_DIM).astype(x.dtype)


def axial_rope_attention(q, k, v, gq, gk, n):
    bsz, L = q.shape[0], q.shape[1]
    nb = n // BLOCK
    g = B_HEADS // B_KV_HEADS
    scale = HEAD_DIM ** -0.5
    cos, sin = axial_rope_angles(n)
    q = apply_axial_rope(head_rms_norm(q, gq), cos, sin)
    k = apply_axial_rope(head_rms_norm(k, gk), cos, sin)
    qg = q.reshape(bsz, L, B_KV_HEADS, g, HEAD_DIM)

    def attend(qb):
        s = jnp.einsum('btkgd,bskd->bkgts', qb, k, preferred_element_type=jnp.float32) * scale
        p = jax.nn.softmax(s, axis=-1).astype(v.dtype)
        return jnp.einsum('bkgts,bskd->btkgd', p, v)

    out_meta = attend(qg[:, :N_META]).reshape(bsz, N_META, B_HEADS * HEAD_DIM)
    q_blocks = jnp.moveaxis(qg[:, N_META:].reshape(bsz, nb, BLOCK, B_KV_HEADS, g, HEAD_DIM), 1, 0)
    out_real = jnp.moveaxis(lax.map(attend, q_blocks), 0, 1).reshape(bsz, n, B_HEADS * HEAD_DIM)
    return jnp.concatenate([out_meta, out_real], axis=1)


def retention_direction(q, k, v, log2_decay, include_diag):
    log_gamma = jnp.log1p(-jnp.exp2(log2_decay.astype(jnp.float32)))
    pos = jnp.arange(CHUNK, dtype=jnp.float32)
    diff = pos[:, None] - pos[None, :]
    mask = diff >= 0 if include_diag else diff > 0
    decay_in = jnp.where(mask, jnp.exp(log_gamma[:, None, None] * jnp.where(mask, diff, 0.0)), 0.0)
    decay_q = jnp.exp(log_gamma[None, :] * (pos[:, None] + 1.0))
    decay_k = jnp.exp(log_gamma[None, :] * (CHUNK - 1.0 - pos[:, None]))
    decay_chunk = jnp.exp(log_gamma * CHUNK)
    scores = jnp.einsum('bnihd,bnjhd->bnhij', q, k) * decay_in
    inner = jnp.einsum('bnhij,bnjhe->bnihe', scores, v)

    def step(state, qkv):
        qc, kc, vc = qkv
        cross = jnp.einsum('bihd,bhde->bihe', qc, state) * decay_q[None, :, :, None]
        state = state * decay_chunk[None, :, None, None] + jnp.einsum('bjhd,bjhe->bhde', kc * decay_k[None, :, :, None], vc)
        return state, cross

    bsz, h, d = q.shape[0], q.shape[3], q.shape[4]
    state0 = jnp.zeros((bsz, h, d, v.shape[-1]), jnp.float32)
    _, cross = lax.scan(step, state0, (jnp.moveaxis(q, 1, 0), jnp.moveaxis(k, 1, 0), jnp.moveaxis(v, 1, 0)))
    return inner + jnp.moveaxis(cross, 0, 1)


def bidirectional_retention(q, k, v, g, log2_decay, gn_w):
    bsz, L, _ = q.shape
    pad = CHUNK - N_META
    lp = L + pad
    nc = lp // CHUNK

    def prep(t, scale):
        t = jnp.pad(t.astype(jnp.float32), ((0, 0), (pad, 0), (0, 0))) * scale
        return t.reshape(bsz, lp, C_HEADS, HEAD_DIM)

    qp, kp, vp = prep(q, 1.0), prep(k, HEAD_DIM ** -0.5), prep(v, 1.0)
    chunk = lambda t: t.reshape(bsz, nc, CHUNK, C_HEADS, HEAD_DIM)
    flip = lambda t: jnp.flip(t, axis=1)
    fwd = retention_direction(chunk(qp), chunk(kp), chunk(vp), log2_decay[0], True)
    bwd = retention_direction(chunk(flip(qp)), chunk(flip(kp)), chunk(flip(vp)), log2_decay[1], False)
    y = fwd.reshape(bsz, lp, C_HEADS, HEAD_DIM) + flip(bwd.reshape(bsz, lp, C_HEADS, HEAD_DIM))
    y = y[:, pad:]
    mu = jnp.mean(y, axis=-1, keepdims=True)
    var = jnp.mean(jnp.square(y - mu), axis=-1, keepdims=True)
    y = ((y - mu) * lax.rsqrt(var + GN_EPS)).reshape(bsz, L, C_WIDTH) * gn_w.astype(jnp.float32)
    return (jax.nn.silu(g.astype(jnp.float32)) * y).astype(q.dtype)


def centred_depthwise_conv(x, w, b):
    out = lax.conv_general_dilated(x, w[:, None, :].astype(x.dtype), window_strides=(1,),
                                   padding=[(CONV_LEFT, CONV_W - 1 - CONV_LEFT)],
                                   dimension_numbers=('NWC', 'WIO', 'NWC'),
                                   feature_group_count=x.shape[-1])
    return out + b.astype(x.dtype)


def rg_lru(x, gate_w, gate_b, lam, reverse):
    bsz, L, w = x.shape
    xf = x.astype(jnp.float32)
    xb = xf.reshape(bsz, L, LRU_BLOCKS, LRU_BLOCK_W)
    gates = jnp.einsum('blnc,gncd->gblnd', xb, gate_w.astype(jnp.float32)).reshape(2, bsz, L, w)
    gates = gates + gate_b.astype(jnp.float32)[:, None, None, :]
    r = jax.nn.sigmoid(gates[0])
    i = jax.nn.sigmoid(gates[1])
    log_a = -LRU_C * r * jax.nn.softplus(-lam.astype(jnp.float32))
    a = jnp.exp(log_a)
    u = jnp.sqrt(-jnp.expm1(2.0 * log_a)) * (i * xf)
    if reverse:
        a, u = jnp.flip(a, axis=1), jnp.flip(u, axis=1)
    _, h = lax.associative_scan(lambda c1, c2: (c1[0] * c2[0], c2[0] * c1[1] + c2[1]), (a, u), axis=1)
    if reverse:
        h = jnp.flip(h, axis=1)
    return h


def hawk_recurrent_branch(x_in, gate_in, conv_w, conv_b, gate_w, gate_b, lam):
    xc = centred_depthwise_conv(x_in, conv_w, conv_b)
    h = rg_lru(xc, gate_w[0], gate_b[0], lam[0], False) + rg_lru(xc, gate_w[1], gate_b[1], lam[1], True)
    return (h * jax.nn.gelu(gate_in.astype(jnp.float32))).astype(x_in.dtype)


def encoder_layer(h, n, norm_mix_w, w_in, attn_sink, qk_norm_q, qk_norm_k, ret_log2_decay, ret_gn_w,
                  lru_conv_w, lru_conv_b, lru_gate_w, lru_gate_b, lru_lambda, w_branch, w_out,
                  norm_mlp_w, w_up, w_down):
    bsz, L, _ = h.shape
    xn = rms_norm(h, norm_mix_w)
    proj = xn @ w_in
    split_at = [int(s) for s in np.cumsum(IN_SPLITS)[:-1]]
    aq, ak, av, bq, bk, bv, cq, ck, cv, cg, dx, dy, gate = jnp.split(proj, split_at, axis=-1)
    heads = lambda t, nh: t.reshape(bsz, L, nh, HEAD_DIM)
    o_a = windowed_sink_attention(heads(aq, A_HEADS), heads(ak, A_KV_HEADS), heads(av, A_KV_HEADS), attn_sink)
    o_b = axial_rope_attention(heads(bq, B_HEADS), heads(bk, B_KV_HEADS), heads(bv, B_KV_HEADS), qk_norm_q, qk_norm_k, n)
    o_c = bidirectional_retention(cq, ck, cv, cg, ret_log2_decay, ret_gn_w)
    o_d = hawk_recurrent_branch(dx, dy, lru_conv_w, lru_conv_b, lru_gate_w, lru_gate_b, lru_lambda)
    branches = jnp.stack([o_a, o_b, o_c, o_d], axis=2)
    projected = jnp.einsum('blnc,ncd->blnd', branches, w_branch)
    gates = jax.nn.sigmoid(gate.astype(jnp.float32)).reshape(bsz, L, N_BRANCH, D_MODEL)
    merged = jnp.einsum('blnd,blnd->bld', projected.astype(jnp.float32), gates).astype(h.dtype)
    h = h + merged @ w_out
    hid = jnp.square(jax.nn.relu(rms_norm(h, norm_mlp_w) @ w_up))
    return h + hid @ w_down


def encoder_trunk(x, meta_tokens, norm_mix_w, w_in, attn_sink, qk_norm_q, qk_norm_k, ret_log2_decay, ret_gn_w,
                  lru_conv_w, lru_conv_b, lru_gate_w, lru_gate_b, lru_lambda, w_branch, w_out,
                  norm_mlp_w, w_up, w_down, final_norm_w):
    bsz, n, _ = x.shape
    meta = jnp.broadcast_to(meta_tokens.astype(x.dtype)[None], (bsz, N_META, D_MODEL))
    h = jnp.concatenate([meta, x], axis=1)
    for l in range(DEPTH):
        h = encoder_layer(h, n, norm_mix_w[l], w_in[l], attn_sink[l], qk_norm_q[l], qk_norm_k[l],
                          ret_log2_decay[l], ret_gn_w[l], lru_conv_w[l], lru_conv_b[l], lru_gate_w[l],
                          lru_gate_b[l], lru_lambda[l], w_branch[l], w_out[l], norm_mlp_w[l], w_up[l], w_down[l])
    return rms_norm(h, final_norm_w)[:, N_META:]


def setup_inputs(seed: int = 0) -> dict:
    key = jax.random.key(seed)
    ks = jax.random.split(key, 24)
    f32 = jnp.float32
    nrm = lambda k, shape, scale: jax.random.normal(k, shape, f32) * scale
    resid_scale = (2.0 * DEPTH) ** -0.5
    x_prompt = nrm(ks[0], (BATCH, SEQ, D_MODEL), 1.0)
    x_sample = nrm(ks[1], (DEC_BATCH, DEC_SEQ, D_MODEL), 1.0)
    meta_tokens = nrm(ks[2], (N_META, D_MODEL), 1.0)
    norm_mix_w = 1.0 + nrm(ks[3], (DEPTH, D_MODEL), 0.02)
    w_in = nrm(ks[4], (DEPTH, D_MODEL, IN_WIDTH), D_MODEL ** -0.5)
    attn_sink = nrm(ks[5], (DEPTH, A_HEADS), 0.5)
    qk_norm_q = 1.0 + nrm(ks[6], (DEPTH, HEAD_DIM), 0.02)
    qk_norm_k = 1.0 + nrm(ks[7], (DEPTH, HEAD_DIM), 0.02)
    ret_log2_decay = (-5.0 - jnp.arange(C_HEADS, dtype=f32))[None, None, :] + nrm(ks[8], (DEPTH, 2, C_HEADS), 0.1)
    ret_gn_w = 1.0 + nrm(ks[9], (DEPTH, C_WIDTH), 0.02)
    lru_conv_w = nrm(ks[10], (DEPTH, CONV_W, LRU_WIDTH), CONV_W ** -0.5)
    lru_conv_b = nrm(ks[11], (DEPTH, LRU_WIDTH), 0.02)
    lru_gate_w = nrm(ks[12], (DEPTH, 2, 2, LRU_BLOCKS, LRU_BLOCK_W, LRU_BLOCK_W), LRU_BLOCK_W ** -0.5)
    lru_gate_b = nrm(ks[13], (DEPTH, 2, 2, LRU_WIDTH), 0.02)
    u = jax.random.uniform(ks[14], (DEPTH, 2, LRU_WIDTH), f32, 0.9, 0.999)
    p = u ** (1.0 / LRU_C)
    lru_lambda = jnp.log(p) - jnp.log1p(-p)
    w_branch = nrm(ks[15], (DEPTH, N_BRANCH, BRANCH_WIDTH, D_MODEL), BRANCH_WIDTH ** -0.5)
    w_out = nrm(ks[16], (DEPTH, D_MODEL, D_MODEL), D_MODEL ** -0.5 * resid_scale)
    norm_mlp_w = 1.0 + nrm(ks[17], (DEPTH, D_MODEL), 0.02)
    w_up = nrm(ks[18], (DEPTH, D_MODEL, D_FF), D_MODEL ** -0.5)
    w_down = nrm(ks[19], (DEPTH, D_FF, D_MODEL), D_FF ** -0.5 * resid_scale)
    final_norm_w = 1.0 + nrm(ks[20], (D_MODEL,), 0.02)
    return {'x_prompt': x_prompt, 'x_sample': x_sample, 'meta_tokens': meta_tokens, 'norm_mix_w': norm_mix_w,
            'w_in': w_in, 'attn_sink': attn_sink, 'qk_norm_q': qk_norm_q, 'qk_norm_k': qk_norm_k,
            'ret_log2_decay': ret_log2_decay, 'ret_gn_w': ret_gn_w, 'lru_conv_w': lru_conv_w,
            'lru_conv_b': lru_conv_b, 'lru_gate_w': lru_gate_w, 'lru_gate_b': lru_gate_b,
            'lru_lambda': lru_lambda, 'w_branch': w_branch, 'w_out': w_out, 'norm_mlp_w': norm_mlp_w,
            'w_up': w_up, 'w_down': w_down, 'final_norm_w': final_norm_w}


def reference(x_prompt, x_sample, meta_tokens, norm_mix_w, w_in, attn_sink, qk_norm_q, qk_norm_k,
              ret_log2_decay, ret_gn_w, lru_conv_w, lru_conv_b, lru_gate_w, lru_gate_b, lru_lambda,
              w_branch, w_out, norm_mlp_w, w_up, w_down, final_norm_w):
    y_prompt = encoder_trunk(x_prompt, meta_tokens, norm_mix_w, w_in, attn_sink, qk_norm_q, qk_norm_k,
                             ret_log2_decay, ret_gn_w, lru_conv_w, lru_conv_b, lru_gate_w, lru_gate_b,
                             lru_lambda, w_branch, w_out, norm_mlp_w, w_up, w_down, final_norm_w)
    y_sample = encoder_trunk(x_sample, meta_tokens, norm_mix_w, w_in, attn_sink, qk_norm_q, qk_norm_k,
                             ret_log2_decay, ret_gn_w, lru_conv_w, lru_conv_b, lru_gate_w, lru_gate_b,
                             lru_lambda, w_branch, w_out, norm_mlp_w, w_up, w_down, final_norm_w)
    return (y_prompt, y_sample)
```

```python
import functools
import math
from typing import NamedTuple

import numpy as np
import jax
import jax.numpy as jnp
from jax import lax
from jax.experimental import pallas as pl
from jax.experimental.pallas import tpu as pltpu

F32 = jnp.float32
BF16 = jnp.bfloat16

HEAD_DIM = 128
N_META = 16
BLK = 128
PAD_ROWS = BLK - N_META
WINDOW = 128
GRID_W = 64
ROPE_THETA = 10000.0
ROPE_FREQS = HEAD_DIM // 4
CONV_LEFT = 2
LRU_C = 8.0
NORM_EPS = 1e-6
GN_EPS = 1e-5
NEG_INF = -1e30
N_BRANCH = 4

V7X_VMEM_BYTES = 64 * 1024 * 1024
MM_TILE = 1024
MM_TILE_K = 2048
FLASH_TK = 512


class Group(NamedTuple):
    base: int
    batch: int
    nb: int
    m: int
    n_real: int

    @property
    def npad(self):
        return self.m * BLK + PAD_ROWS

    @property
    def rows(self):
        return self.nb * BLK


def _cparams(sem, vmem_mib):
    assert vmem_mib * 1024 * 1024 < V7X_VMEM_BYTES
    return pltpu.CompilerParams(dimension_semantics=sem, vmem_limit_bytes=vmem_mib * 1024 * 1024)


def _largest_tile(n, cap):
    t = cap
    while n % t:
        t //= 2
    return t


def _rmsnorm_body(x_ref, w_ref, o_ref):
    x = x_ref[...].astype(F32)
    ms = jnp.mean(x * x, axis=-1, keepdims=True)
    o_ref[...] = ((x * lax.rsqrt(ms + NORM_EPS)) * w_ref[...]).astype(o_ref.dtype)


def _rmsnorm(h, w, out_dtype):
    t, d = h.shape
    tr = _largest_tile(t, 256)
    return pl.pallas_call(
        _rmsnorm_body,
        grid=(t // tr,),
        in_specs=[pl.BlockSpec((tr, d), lambda i: (i, 0)), pl.BlockSpec((1, d), lambda i: (0, 0))],
        out_specs=pl.BlockSpec((tr, d), lambda i: (i, 0)),
        out_shape=jax.ShapeDtypeStruct((t, d), out_dtype),
        compiler_params=_cparams(("parallel",), 32),
        name="rmsnorm",
    )(h, w.reshape(1, d).astype(F32))


def _final_norm(h, w, grp, out_dtype):
    _, d = h.shape
    nrb = grp.n_real // BLK
    return pl.pallas_call(
        _rmsnorm_body,
        grid=(grp.batch, nrb),
        in_specs=[pl.BlockSpec((BLK, d), lambda b, j: (grp.base + b * grp.nb + grp.m + 1 + j, 0)),
                  pl.BlockSpec((1, d), lambda b, j: (0, 0))],
        out_specs=pl.BlockSpec((None, BLK, d), lambda b, j: (b, j, 0)),
        out_shape=jax.ShapeDtypeStruct((grp.batch, grp.n_real, d), out_dtype),
        compiler_params=_cparams(("parallel", "parallel"), 32),
        name="final_norm",
    )(h, w.reshape(1, d).astype(F32))


def _mm_body(l_ref, a_ref, b_ref, *rest, epilogue, nk):
    del l_ref
    if epilogue == "residual":
        r_ref, o_ref, *scratch = rest
    else:
        o_ref, *scratch = rest

    def finish(acc):
        if epilogue == "relu2":
            acc = jnp.square(jnp.maximum(acc, 0.0))
        elif epilogue == "residual":
            acc = r_ref[...] + acc
        o_ref[...] = acc.astype(o_ref.dtype)

    prod = jnp.dot(a_ref[...], b_ref[...], preferred_element_type=F32)
    if nk == 1:
        finish(prod)
    else:
        acc_ref, = scratch
        k = pl.program_id(2)

        @pl.when(k == 0)
        def _():
            acc_ref[...] = prod

        @pl.when(k > 0)
        def _():
            acc_ref[...] += prod

        @pl.when(k == nk - 1)
        def _():
            finish(acc_ref[...])


def _matmul(layer, a, w, *, epilogue="none", residual=None, out_dtype=BF16):
    t, kdim = a.shape
    n = w.shape[-1]
    tm = _largest_tile(t, MM_TILE)
    tn = _largest_tile(n, MM_TILE)
    tk = kdim if kdim <= 2 * MM_TILE_K else _largest_tile(kdim, MM_TILE_K)
    nk = kdim // tk
    in_specs = [pl.BlockSpec((tm, tk), lambda i, j, k, l: (i, k)),
                pl.BlockSpec((None, tk, tn), lambda i, j, k, l: (l[0], k, j))]
    args = [a, w]
    aliases = {}
    if epilogue == "residual":
        in_specs.append(pl.BlockSpec((tm, tn), lambda i, j, k, l: (i, j)))
        args.append(residual)
        aliases = {3: 0}
    return pl.pallas_call(
        functools.partial(_mm_body, epilogue=epilogue, nk=nk),
        grid_spec=pltpu.PrefetchScalarGridSpec(
            num_scalar_prefetch=1,
            grid=(t // tm, n // tn, nk),
            in_specs=in_specs,
            out_specs=pl.BlockSpec((tm, tn), lambda i, j, k, l: (i, j)),
            scratch_shapes=[pltpu.VMEM((tm, tn), F32)] if nk > 1 else [],
        ),
        out_shape=jax.ShapeDtypeStruct((t, n), out_dtype),
        input_output_aliases=aliases,
        compiler_params=_cparams(("parallel", "parallel", "arbitrary"), 56),
        name="matmul_" + epilogue,
    )(layer, *args)


def _merge_body(l_ref, mask_ref, oa_ref, ob_ref, oc_ref, od_ref, wb_ref, ga_ref, gb_ref, gc_ref, gd_ref, out_ref):
    del l_ref
    acc = None
    for b, (o_ref, g_ref) in enumerate(((oa_ref, ga_ref), (ob_ref, gb_ref), (oc_ref, gc_ref), (od_ref, gd_ref))):
        proj = jnp.dot(o_ref[...], wb_ref[b], preferred_element_type=F32)
        term = proj * jax.nn.sigmoid(g_ref[...].astype(F32))
        acc = term if acc is None else acc + term
    out_ref[...] = jnp.where(mask_ref[...] > 0.0, acc, 0.0).astype(out_ref.dtype)


def _merge(layer, row_mask, outs, w_branch, proj, gate_col0):
    t, bw = outs[0].shape
    d = w_branch.shape[-1]
    tm = _largest_tile(t, MM_TILE)
    tn = _largest_tile(d, 512)
    assert gate_col0 % tn == 0 and d % tn == 0
    o_spec = pl.BlockSpec((tm, bw), lambda i, j, l: (i, 0))

    def gate_spec(b):
        return pl.BlockSpec((tm, tn), lambda i, j, l: (i, (gate_col0 + b * d) // tn + j))

    return pl.pallas_call(
        _merge_body,
        grid_spec=pltpu.PrefetchScalarGridSpec(
            num_scalar_prefetch=1,
            grid=(t // tm, d // tn),
            in_specs=[pl.BlockSpec((tm, 1), lambda i, j, l: (i, 0)), o_spec, o_spec, o_spec, o_spec,
                      pl.BlockSpec((None, N_BRANCH, bw, tn), lambda i, j, l: (l[0], 0, 0, j)),
                      gate_spec(0), gate_spec(1), gate_spec(2), gate_spec(3)],
            out_specs=pl.BlockSpec((tm, tn), lambda i, j, l: (i, j)),
        ),
        out_shape=jax.ShapeDtypeStruct((t, d), BF16),
        compiler_params=_cparams(("parallel", "arbitrary"), 56),
        name="branch_merge",
    )(layer, row_mask, *outs, w_branch, proj, proj, proj, proj)


def _attn_a_body(sink_ref, q_ref, km_ref, kp_ref, kc_ref, kn_ref, vm_ref, vp_ref, vc_ref, vn_ref, o_ref,
                 *, m, nb, heads, kv):
    j = pl.program_id(1)
    g = heads // kv
    scale = HEAD_DIM ** -0.5
    shape = (BLK, 4 * BLK)
    ri = lax.broadcasted_iota(jnp.int32, shape, 0)
    ci = lax.broadcasted_iota(jnp.int32, shape, 1)
    slot = lax.shift_right_logical(ci, 7)
    kin = lax.bitwise_and(ci, BLK - 1)
    dist = jnp.abs(ri - kin + (2 - slot) * BLK)
    kblk = j + slot - 2
    band_ok = (slot >= 1) & (kblk >= m + 1) & (kblk <= nb - 1) & (dist <= WINDOW)
    meta_ok = (slot == 0) & (kin >= PAD_ROWS)
    valid = band_ok | meta_ok
    distf = jnp.where(slot >= 1, dist, 0).astype(F32)
    for kh in range(kv):
        cs = slice(kh * HEAD_DIM, (kh + 1) * HEAD_DIM)
        kcat = jnp.concatenate([km_ref[:, cs], kp_ref[:, cs], kc_ref[:, cs], kn_ref[:, cs]], axis=0)
        vcat = jnp.concatenate([vm_ref[:, cs], vp_ref[:, cs], vc_ref[:, cs], vn_ref[:, cs]], axis=0)
        for gi in range(g):
            h = kh * g + gi
            hs = slice(h * HEAD_DIM, (h + 1) * HEAD_DIM)
            slope = 2.0 ** (-8.0 * (h + 1) / heads)
            s = lax.dot_general(q_ref[:, hs], kcat, (((1,), (1,)), ((), ())), preferred_element_type=F32) * scale
            s = jnp.where(valid, s - slope * distf, NEG_INF)
            sink = sink_ref[h]
            mx = jnp.maximum(jnp.max(s, axis=-1, keepdims=True), sink)
            p = jnp.exp(s - mx)
            den = jnp.sum(p, axis=-1, keepdims=True) + jnp.exp(sink - mx)
            p = (p / den).astype(BF16)
            o_ref[:, hs] = jnp.dot(p, vcat, preferred_element_type=F32).astype(o_ref.dtype)


def _attn_a(proj, sink, grp, heads, kv, col_q, col_k, col_v):
    qw, kw = heads * HEAD_DIM, kv * HEAD_DIM
    assert col_q % heads == 0 and col_k % kv == 0 and col_v % kv == 0
    base, nb, m = grp.base, grp.nb, grp.m

    def kv_spec(col, which):
        def imap(b, j):
            if which == "meta":
                jj = m
            elif which == "prev":
                jj = jnp.maximum(j - 1, 0)
            elif which == "next":
                jj = jnp.minimum(j + 1, nb - 1)
            else:
                jj = j
            return (base + b * nb + jj, col // kv)
        return pl.BlockSpec((BLK, kw), imap)

    order = ("meta", "prev", "own", "next")
    return pl.pallas_call(
        functools.partial(_attn_a_body, m=m, nb=nb, heads=heads, kv=kv),
        grid=(grp.batch, nb),
        in_specs=[pl.BlockSpec(memory_space=pltpu.SMEM),
                  pl.BlockSpec((BLK, qw), lambda b, j: (base + b * nb + j, col_q // heads))]
                 + [kv_spec(col_k, w) for w in order] + [kv_spec(col_v, w) for w in order],
        out_specs=pl.BlockSpec((BLK, qw), lambda b, j: (b * nb + j, 0)),
        out_shape=jax.ShapeDtypeStruct((grp.batch * grp.rows, qw), BF16),
        compiler_params=_cparams(("parallel", "parallel"), 32),
        name="attn_window",
    )(sink.astype(F32), *([proj] * 9))


def _rope_tables(grp):
    r = np.arange(grp.rows)
    t = r - grp.npad - N_META
    row = np.where(t >= 0, t // GRID_W, -1)
    col = np.where(t >= 0, t % GRID_W, np.clip(r - grp.npad, 0, N_META - 1))
    pos = jnp.asarray(np.stack([row, col], axis=-1), F32)
    inv_freq = ROPE_THETA ** (-jnp.arange(ROPE_FREQS, dtype=F32) / ROPE_FREQS)
    ang = pos[:, :, None] * inv_freq
    cos, sin = jnp.cos(ang), jnp.sin(ang)
    cos_t = jnp.concatenate([cos[:, 0], cos[:, 0], cos[:, 1], cos[:, 1]], axis=-1)
    sin_t = jnp.concatenate([-sin[:, 0], sin[:, 0], -sin[:, 1], sin[:, 1]], axis=-1)
    return cos_t, sin_t


def _prep_b_body(q_ref, k_ref, v_ref, cos_ref, sin_ref, gq_ref, gk_ref, qo_ref, ko_ref, vo_ref, *, heads, kv):
    cos, sin = cos_ref[...], sin_ref[...]
    lane = lax.broadcasted_iota(jnp.int32, (BLK, HEAD_DIM), 1)
    first_half = lax.bitwise_and(lane, 2 * ROPE_FREQS - 1) < ROPE_FREQS

    def norm_rope(x, w):
        x = x.astype(F32)
        ms = jnp.mean(x * x, axis=-1, keepdims=True)
        xn = (x * lax.rsqrt(ms + NORM_EPS)) * w
        partner = jnp.where(first_half, pltpu.roll(xn, HEAD_DIM - ROPE_FREQS, axis=1), pltpu.roll(xn, ROPE_FREQS, axis=1))
        return xn * cos + partner * sin

    for h in range(heads):
        hs = slice(h * HEAD_DIM, (h + 1) * HEAD_DIM)
        qo_ref[:, hs] = norm_rope(q_ref[:, hs], gq_ref[...]).astype(qo_ref.dtype)
    for h in range(kv):
        hs = slice(h * HEAD_DIM, (h + 1) * HEAD_DIM)
        ko_ref[:, hs] = norm_rope(k_ref[:, hs], gk_ref[...]).astype(ko_ref.dtype)
    vo_ref[...] = v_ref[...]


def _prep_b(proj, gq, gk, grp, heads, kv, col_q, col_k, col_v):
    qw, kw = heads * HEAD_DIM, kv * HEAD_DIM
    assert col_q % heads == 0 and col_k % kv == 0 and col_v % kv == 0
    base, nb = grp.base, grp.nb
    cos_t, sin_t = _rope_tables(grp)
    tab_spec = pl.BlockSpec((BLK, HEAD_DIM), lambda b, j: (j, 0))
    vec_spec = pl.BlockSpec((1, HEAD_DIM), lambda b, j: (0, 0))

    def in_spec(width, col, per):
        return pl.BlockSpec((BLK, width), lambda b, j: (base + b * nb + j, col // per))

    def out_spec(width):
        return pl.BlockSpec((None, BLK, width), lambda b, j: (b, j, 0))

    return pl.pallas_call(
        functools.partial(_prep_b_body, heads=heads, kv=kv),
        grid=(grp.batch, nb),
        in_specs=[in_spec(qw, col_q, heads), in_spec(kw, col_k, kv), in_spec(kw, col_v, kv),
                  tab_spec, tab_spec, vec_spec, vec_spec],
        out_specs=[out_spec(qw), out_spec(kw), out_spec(kw)],
        out_shape=[jax.ShapeDtypeStruct((grp.batch, grp.rows, qw), BF16),
                   jax.ShapeDtypeStruct((grp.batch, grp.rows, kw), BF16),
                   jax.ShapeDtypeStruct((grp.batch, grp.rows, kw), BF16)],
        compiler_params=_cparams(("parallel", "parallel"), 32),
        name="attn_axial_prep",
    )(proj, proj, proj, cos_t, sin_t, gq.reshape(1, HEAD_DIM).astype(F32), gk.reshape(1, HEAD_DIM).astype(F32))


def _flash_b_body(q_ref, k_ref, v_ref, o_ref, *, m, n_real, tk, g):
    rows = g * BLK
    scale = HEAD_DIM ** -0.5
    q = jnp.concatenate([q_ref[:, i * HEAD_DIM:(i + 1) * HEAD_DIM] for i in range(g)], axis=0)

    def step(kc, vc, carry, mask):
        mx, den, acc = carry
        s = lax.dot_general(q, kc, (((1,), (1,)), ((), ())), preferred_element_type=F32) * scale
        if mask is not None:
            s = jnp.where(mask, s, NEG_INF)
        mnew = jnp.maximum(mx, jnp.max(s, axis=-1, keepdims=True))
        alpha = jnp.exp(mx - mnew)
        p = jnp.exp(s - mnew)
        den = alpha * den + jnp.sum(p, axis=-1, keepdims=True)
        acc = alpha * acc + jnp.dot(p.astype(BF16), vc, preferred_element_type=F32)
        return mnew, den, acc

    carry = (jnp.full((rows, 1), NEG_INF, F32), jnp.zeros((rows, 1), F32), jnp.zeros((rows, HEAD_DIM), F32))
    meta_mask = lax.broadcasted_iota(jnp.int32, (rows, BLK), 1) >= PAD_ROWS
    carry = step(k_ref[m * BLK:(m + 1) * BLK, :], v_ref[m * BLK:(m + 1) * BLK, :], carry, meta_mask)

    def body(c, carry):
        start = pl.multiple_of((m + 1) * BLK + c * tk, BLK)
        return step(k_ref[pl.ds(start, tk), :], v_ref[pl.ds(start, tk), :], carry, None)

    _, den, acc = lax.fori_loop(0, n_real // tk, body, carry)
    out = acc / den
    o_ref[...] = jnp.concatenate([out[i * BLK:(i + 1) * BLK] for i in range(g)], axis=1).astype(o_ref.dtype)


def _flash_b(q, k, v, grp, heads, kv):
    g = heads // kv
    tk = _largest_tile(grp.n_real, FLASH_TK)
    seq_spec = pl.BlockSpec((None, grp.rows, HEAD_DIM), lambda b, kh, j: (b, 0, kh))
    out = pl.pallas_call(
        functools.partial(_flash_b_body, m=grp.m, n_real=grp.n_real, tk=tk, g=g),
        grid=(grp.batch, kv, grp.nb),
        in_specs=[pl.BlockSpec((None, BLK, g * HEAD_DIM), lambda b, kh, j: (b, j, kh)), seq_spec, seq_spec],
        out_specs=pl.BlockSpec((None, BLK, g * HEAD_DIM), lambda b, kh, j: (b, j, kh)),
        out_shape=jax.ShapeDtypeStruct((grp.batch, grp.rows, heads * HEAD_DIM), BF16),
        compiler_params=_cparams(("parallel", "parallel", "arbitrary"), 48),
        name="attn_axial",
    )(q, k, v)
    return out.reshape(grp.batch * grp.rows, heads * HEAD_DIM)


def _ret_tables(l2d_ref, heads, tab_in, tab_q, tab_k, *, backward_only):
    ri = lax.broadcasted_iota(jnp.int32, (BLK, BLK), 0).astype(F32)
    ci = lax.broadcasted_iota(jnp.int32, (BLK, BLK), 1).astype(F32)
    for h in range(heads):
        lg_b = jnp.log1p(-jnp.exp2(l2d_ref[heads + h:heads + h + 1, :]))
        if backward_only:
            tab_q[h] = jnp.exp(lg_b * (BLK - ri))
            tab_k[h] = jnp.exp(lg_b * ri)
        else:
            lg_f = jnp.log1p(-jnp.exp2(l2d_ref[h:h + 1, :]))
            diff = ri - ci
            fwd = jnp.exp(lg_f * jnp.maximum(diff, 0.0))
            bwd = jnp.exp(lg_b * jnp.maximum(-diff, 0.0))
            tab_in[h] = jnp.where(diff >= 0, fwd, bwd)
            tab_q[h] = jnp.exp(lg_f * (ri + 1.0))
            tab_k[h] = jnp.exp(lg_f * (BLK - 1.0 - ri))


def _ret_bwd_body(l2d_ref, q_ref, k_ref, v_ref, y_ref, state, tab_q, tab_k, *, heads, nb, npad):
    b, j = pl.program_id(0), pl.program_id(1)
    jj = nb - 1 - j
    scale = HEAD_DIM ** -0.5

    @pl.when((b == 0) & (j == 0))
    def _():
        _ret_tables(l2d_ref, heads, None, tab_q, tab_k, backward_only=True)

    @pl.when(j == 0)
    def _():
        state[...] = jnp.zeros_like(state)

    valid = (jj * BLK + lax.broadcasted_iota(jnp.int32, (BLK, HEAD_DIM), 0)) >= npad
    for h in range(heads):
        hs = slice(h * HEAD_DIM, (h + 1) * HEAD_DIM)
        ks = jnp.where(valid, k_ref[:, hs].astype(F32) * scale, 0.0)
        vh = jnp.where(valid, v_ref[:, hs], jnp.zeros((), v_ref.dtype))
        st = state[h]
        y_ref[:, hs] = jnp.dot(q_ref[:, hs], st.astype(BF16), preferred_element_type=F32) * tab_q[h]
        kd = (ks * tab_k[h]).astype(BF16)
        upd = lax.dot_general(kd, vh, (((0,), (0,)), ((), ())), preferred_element_type=F32)
        state[h] = st * tab_q[h][0:1, :] + upd


def _ret_fwd_body(l2d_ref, q_ref, k_ref, v_ref, g_ref, yb_ref, gn_ref, o_ref, state, tab_in, tab_q, tab_k,
                  *, heads, npad):
    b, j = pl.program_id(0), pl.program_id(1)
    scale = HEAD_DIM ** -0.5

    @pl.when((b == 0) & (j == 0))
    def _():
        _ret_tables(l2d_ref, heads, tab_in, tab_q, tab_k, backward_only=False)

    @pl.when(j == 0)
    def _():
        state[...] = jnp.zeros_like(state)

    valid = (j * BLK + lax.broadcasted_iota(jnp.int32, (BLK, HEAD_DIM), 0)) >= npad
    for h in range(heads):
        hs = slice(h * HEAD_DIM, (h + 1) * HEAD_DIM)
        qh = q_ref[:, hs]
        ks = jnp.where(valid, k_ref[:, hs].astype(F32) * scale, 0.0)
        vh = jnp.where(valid, v_ref[:, hs], jnp.zeros((), v_ref.dtype))
        st = state[h]
        scores = lax.dot_general(qh, ks.astype(BF16), (((1,), (1,)), ((), ())), preferred_element_type=F32) * tab_in[h]
        inner = jnp.dot(scores.astype(BF16), vh, preferred_element_type=F32)
        cross = jnp.dot(qh, st.astype(BF16), preferred_element_type=F32) * tab_q[h]
        kd = (ks * tab_k[h]).astype(BF16)
        upd = lax.dot_general(kd, vh, (((0,), (0,)), ((), ())), preferred_element_type=F32)
        state[h] = st * tab_q[h][BLK - 1:BLK, :] + upd
        y = inner + cross + yb_ref[:, hs]
        mu = jnp.mean(y, axis=-1, keepdims=True)
        yc = y - mu
        var = jnp.mean(yc * yc, axis=-1, keepdims=True)
        yn = (yc * lax.rsqrt(var + GN_EPS)) * gn_ref[:, hs]
        o_ref[:, hs] = (jax.nn.silu(g_ref[:, hs].astype(F32)) * yn).astype(o_ref.dtype)


def _retention(proj, l2d, gn_w, grp, heads, col_q, col_k, col_v, col_g):
    width = heads * HEAD_DIM
    assert all(c % heads == 0 for c in (col_q, col_k, col_v, col_g))
    base, nb = grp.base, grp.nb
    l2d_rows = jnp.broadcast_to(l2d.astype(F32).reshape(2 * heads, 1), (2 * heads, HEAD_DIM))
    l2d_spec = pl.BlockSpec((2 * heads, HEAD_DIM), lambda b, j: (0, 0))
    table = pltpu.VMEM((heads, BLK, BLK), F32)

    def in_spec(col, reverse):
        def imap(b, j):
            jj = nb - 1 - j if reverse else j
            return (base + b * nb + jj, col // heads)
        return pl.BlockSpec((BLK, width), imap)

    y_bwd = pl.pallas_call(
        functools.partial(_ret_bwd_body, heads=heads, nb=nb, npad=grp.npad),
        grid=(grp.batch, nb),
        in_specs=[l2d_spec, in_spec(col_q, True), in_spec(col_k, True), in_spec(col_v, True)],
        out_specs=pl.BlockSpec((BLK, width), lambda b, j: (b * nb + nb - 1 - j, 0)),
        out_shape=jax.ShapeDtypeStruct((grp.batch * grp.rows, width), F32),
        scratch_shapes=[table, table, table],
        compiler_params=_cparams(("arbitrary", "arbitrary"), 32),
        name="retention_bwd",
    )(l2d_rows, proj, proj, proj)
    return pl.pallas_call(
        functools.partial(_ret_fwd_body, heads=heads, npad=grp.npad),
        grid=(grp.batch, nb),
        in_specs=[l2d_spec, in_spec(col_q, False), in_spec(col_k, False), in_spec(col_v, False),
                  in_spec(col_g, False), pl.BlockSpec((BLK, width), lambda b, j: (b * nb + j, 0)),
                  pl.BlockSpec((1, width), lambda b, j: (0, 0))],
        out_specs=pl.BlockSpec((BLK, width), lambda b, j: (b * nb + j, 0)),
        out_shape=jax.ShapeDtypeStruct((grp.batch * grp.rows, width), BF16),
        scratch_shapes=[table, table, table, table],
        compiler_params=_cparams(("arbitrary", "arbitrary"), 32),
        name="retention_fwd",
    )(l2d_rows, proj, proj, proj, proj, y_bwd, gn_w.reshape(1, width).astype(F32))


def _expm1(x):
    u = jnp.exp(x)
    um1 = u - 1.0
    lu = jnp.log(u)
    r = um1 * x / jnp.where(lu == 0.0, 1.0, lu)
    return jnp.where(u == 1.0, x, jnp.where(um1 == -1.0, -1.0, r))


def _softplus(x):
    return jnp.maximum(x, 0.0) + jnp.log1p(jnp.exp(-jnp.abs(x)))


def _lru_block(x_ref, prev_ref, next_ref, cw_ref, cb_ref, gw_ref, gb_ref, lam_ref, carry_ref, jj, *,
               nblocks, npad, seq_rows, reverse):
    width = x_ref.shape[-1]
    tail = prev_ref.shape[0]
    ri = lax.broadcasted_iota(jnp.int32, (BLK, width), 0)
    pos = jj * BLK + ri
    valid = pos >= npad
    x = jnp.where(valid, x_ref[...].astype(F32), 0.0)
    pi = jj * BLK - tail + lax.broadcasted_iota(jnp.int32, (tail, width), 0)
    prev = jnp.where(pi >= npad, prev_ref[...].astype(F32), 0.0)
    ni = (jj + 1) * BLK + lax.broadcasted_iota(jnp.int32, (tail, width), 0)
    nxt = jnp.where(ni < seq_rows, next_ref[...].astype(F32), 0.0)
    p1 = prev[tail - 1:tail, :]
    p2 = prev[tail - 2:tail - 1, :]
    n0 = nxt[0:1, :]
    x_m1 = jnp.where(ri == 0, p1, pltpu.roll(x, 1, axis=0))
    x_m2 = jnp.where(ri == 0, p2, jnp.where(ri == 1, p1, pltpu.roll(x, 2, axis=0)))
    x_p1 = jnp.where(ri == BLK - 1, n0, pltpu.roll(x, BLK - 1, axis=0))
    xc = (cw_ref[0:1, :] * x_m2 + cw_ref[1:2, :] * x_m1 + cw_ref[2:3, :] * x + cw_ref[3:4, :] * x_p1) + cb_ref[...]
    xcb = xc.astype(BF16)
    rl = lax.broadcasted_iota(jnp.int32, (BLK, HEAD_DIM), 0)
    ok = valid[:, :HEAD_DIM]
    outs = []
    for c in range(nblocks):
        cs = slice(c * HEAD_DIM, (c + 1) * HEAD_DIM)
        gr = jnp.dot(xcb[:, cs], gw_ref[0, c], preferred_element_type=F32) + gb_ref[0:1, cs]
        gi = jnp.dot(xcb[:, cs], gw_ref[1, c], preferred_element_type=F32) + gb_ref[1:2, cs]
        r = jax.nn.sigmoid(gr)
        i = jax.nn.sigmoid(gi)
        log_a = (-LRU_C * r) * _softplus(-lam_ref[:, cs])
        a = jnp.where(ok, jnp.exp(log_a), 1.0)
        u = jnp.where(ok, jnp.sqrt(-_expm1(2.0 * log_a)) * (i * xc[:, cs]), 0.0)
        sh = 1
        while sh < BLK:
            if reverse:
                keep = rl < BLK - sh
                a_s = pltpu.roll(a, BLK - sh, axis=0)
                u_s = pltpu.roll(u, BLK - sh, axis=0)
            else:
                keep = rl >= sh
                a_s = pltpu.roll(a, sh, axis=0)
                u_s = pltpu.roll(u, sh, axis=0)
            u = jnp.where(keep, a * u_s + u, u)
            a = jnp.where(keep, a * a_s, a)
            sh *= 2
        hcol = u + a * carry_ref[:, cs]
        edge = 0 if reverse else BLK - 1
        carry_ref[:, cs] = hcol[edge:edge + 1, :]
        outs.append(hcol)
    return outs


def _lru_bwd_body(x_ref, prev_ref, next_ref, cw_ref, cb_ref, gw_ref, gb_ref, lam_ref, h_ref, carry_ref,
                  *, nb, nblocks, npad):
    j = pl.program_id(1)

    @pl.when(j == 0)
    def _():
        carry_ref[...] = jnp.zeros_like(carry_ref)

    outs = _lru_block(x_ref, prev_ref, next_ref, cw_ref, cb_ref, gw_ref, gb_ref, lam_ref, carry_ref, nb - 1 - j,
                      nblocks=nblocks, npad=npad, seq_rows=nb * BLK, reverse=True)
    for c, hcol in enumerate(outs):
        h_ref[:, c * HEAD_DIM:(c + 1) * HEAD_DIM] = hcol


def _lru_fwd_body(x_ref, prev_ref, next_ref, y_ref, hb_ref, cw_ref, cb_ref, gw_ref, gb_ref, lam_ref, o_ref, carry_ref,
                  *, nb, nblocks, npad):
    j = pl.program_id(1)

    @pl.when(j == 0)
    def _():
        carry_ref[...] = jnp.zeros_like(carry_ref)

    outs = _lru_block(x_ref, prev_ref, next_ref, cw_ref, cb_ref, gw_ref, gb_ref, lam_ref, carry_ref, j,
                      nblocks=nblocks, npad=npad, seq_rows=nb * BLK, reverse=False)
    for c, hcol in enumerate(outs):
        cs = slice(c * HEAD_DIM, (c + 1) * HEAD_DIM)
        gate = jax.nn.gelu(y_ref[:, cs].astype(F32))
        o_ref[:, cs] = ((hcol + hb_ref[:, cs]) * gate).astype(o_ref.dtype)


def _hawk(proj, conv_w, conv_b, gate_w, gate_b, lam, grp, total_rows, col_x, col_y):
    nblocks = gate_w.shape[2]
    width = nblocks * HEAD_DIM
    tail = 16
    per = BLK // tail
    assert col_x % nblocks == 0 and col_y % nblocks == 0
    base, nb = grp.base, grp.nb
    last_tail = total_rows // tail - 1

    def specs(reverse):
        def blk(b, j):
            return base + b * nb + (nb - 1 - j if reverse else j)
        return [pl.BlockSpec((BLK, width), lambda b, j: (blk(b, j), col_x // nblocks)),
                pl.BlockSpec((tail, width), lambda b, j: (jnp.maximum(blk(b, j) * per - 1, 0), col_x // nblocks)),
                pl.BlockSpec((tail, width), lambda b, j: (jnp.minimum((blk(b, j) + 1) * per, last_tail), col_x // nblocks))]

    def param_specs(d):
        return [pl.BlockSpec((conv_w.shape[0], width), lambda b, j: (0, 0)),
                pl.BlockSpec((1, width), lambda b, j: (0, 0)),
                pl.BlockSpec((None, 2, nblocks, HEAD_DIM, HEAD_DIM), lambda b, j: (d, 0, 0, 0, 0)),
                pl.BlockSpec((None, 2, width), lambda b, j: (d, 0, 0)),
                pl.BlockSpec((None, 1, width), lambda b, j: (d, 0, 0))]

    params = (conv_w.astype(F32), conv_b.reshape(1, width).astype(F32), gate_w, gate_b.astype(F32),
              lam.reshape(2, 1, width).astype(F32))
    carry = pltpu.VMEM((1, width), F32)
    h_bwd = pl.pallas_call(
        functools.partial(_lru_bwd_body, nb=nb, nblocks=nblocks, npad=grp.npad),
        grid=(grp.batch, nb),
        in_specs=specs(True) + param_specs(1),
        out_specs=pl.BlockSpec((BLK, width), lambda b, j: (b * nb + nb - 1 - j, 0)),
        out_shape=jax.ShapeDtypeStruct((grp.batch * grp.rows, width), F32),
        scratch_shapes=[carry],
        compiler_params=_cparams(("arbitrary", "arbitrary"), 32),
        name="rglru_bwd",
    )(proj, proj, proj, *params)
    return pl.pallas_call(
        functools.partial(_lru_fwd_body, nb=nb, nblocks=nblocks, npad=grp.npad),
        grid=(grp.batch, nb),
        in_specs=specs(False)
                 + [pl.BlockSpec((BLK, width), lambda b, j: (base + b * nb + j, col_y // nblocks)),
                    pl.BlockSpec((BLK, width), lambda b, j: (b * nb + j, 0))]
                 + param_specs(0),
        out_specs=pl.BlockSpec((BLK, width), lambda b, j: (b * nb + j, 0)),
        out_shape=jax.ShapeDtypeStruct((grp.batch * grp.rows, width), BF16),
        scratch_shapes=[carry],
        compiler_params=_cparams(("arbitrary", "arbitrary"), 32),
        name="rglru_fwd",
    )(proj, proj, proj, proj, h_bwd, *params)


def _plan(batches, seqs):
    per_tile = MM_TILE // BLK
    nbs = [1 + s // BLK for s in seqs]
    total = sum(b * nb for b, nb in zip(batches, nbs))
    extra = (-total) % per_tile
    m0 = extra // batches[0] if extra % batches[0] == 0 else 0
    groups, base = [], 0
    for gi, (b, s, nb) in enumerate(zip(batches, seqs, nbs)):
        m = m0 if gi == 0 else 0
        groups.append(Group(base=base, batch=b, nb=nb + m, m=m, n_real=s))
        base += b * (nb + m)
    return groups, base * BLK


def kernel(x_prompt, x_sample, meta_tokens, norm_mix_w, w_in, attn_sink, qk_norm_q, qk_norm_k, ret_log2_decay,
           ret_gn_w, lru_conv_w, lru_conv_b, lru_gate_w, lru_gate_b, lru_lambda, w_branch, w_out, norm_mlp_w,
           w_up, w_down, final_norm_w):
    xs = (x_prompt, x_sample)
    d = x_prompt.shape[-1]
    depth = w_in.shape[0]
    bw = w_branch.shape[2]
    heads = bw // HEAD_DIM
    kv = heads // 4
    assert all(x.shape[1] % BLK == 0 and x.shape[1] % GRID_W == 0 for x in xs) and bw % HEAD_DIM == 0
    groups, total_rows = _plan([x.shape[0] for x in xs], [x.shape[1] for x in xs])

    parts, mask_parts = [], []
    for grp, x in zip(groups, xs):
        pad = jnp.zeros((grp.batch, grp.npad, d), x.dtype)
        meta = jnp.broadcast_to(meta_tokens.astype(x.dtype)[None], (grp.batch, N_META, d))
        parts.append(jnp.concatenate([pad, meta, x], axis=1).reshape(grp.batch * grp.rows, d))
        mask_parts.append(np.tile(np.arange(grp.rows) >= grp.npad, grp.batch))
    h = jnp.concatenate(parts, axis=0).astype(F32)
    row_mask = jnp.asarray(np.concatenate(mask_parts).astype(np.float32).reshape(total_rows, 1))

    hw, kw = heads * HEAD_DIM, kv * HEAD_DIM
    widths = (hw, kw, kw, hw, kw, kw, bw, bw, bw, bw, bw, bw, N_BRANCH * d)
    assert sum(widths) == w_in.shape[-1]
    offs = np.concatenate([[0], np.cumsum(widths)])
    cols = [w_in[:, :, int(offs[i]):int(offs[i + 1])] for i in range(len(widths))]
    order = (0, 3, 1, 2, 4, 5, 6, 7, 8, 9, 10, 11, 12)
    w_in_b = jnp.concatenate([cols[i] for i in order], axis=-1).astype(BF16)
    col, names = {}, ("aq", "bq", "ak", "av", "bk", "bv", "cq", "ck", "cv", "cg", "dx", "dy", "gate")
    acc = 0
    for name, i in zip(names, order):
        col[name] = acc // HEAD_DIM
        acc += widths[i]
    w_branch_b, w_out_b, w_up_b, w_down_b = (w.astype(BF16) for w in (w_branch, w_out, w_up, w_down))
    gate_w_b = lru_gate_w.astype(BF16)

    def layer_fn(l, h):
        layer = jnp.full((1,), l, jnp.int32)
        at = lambda p: lax.dynamic_index_in_dim(p, l, axis=0, keepdims=False)
        xn = _rmsnorm(h, at(norm_mix_w), BF16)
        proj = _matmul(layer, xn, w_in_b)
        oa, ob, oc, od = [], [], [], []
        for grp in groups:
            oa.append(_attn_a(proj, at(attn_sink), grp, heads, kv, col["aq"], col["ak"], col["av"]))
            qb, kb, vb = _prep_b(proj, at(qk_norm_q), at(qk_norm_k), grp, heads, kv, col["bq"], col["bk"], col["bv"])
            ob.append(_flash_b(qb, kb, vb, grp, heads, kv))
            oc.append(_retention(proj, at(ret_log2_decay), at(ret_gn_w), grp, heads,
                                 col["cq"], col["ck"], col["cv"], col["cg"]))
            od.append(_hawk(proj, at(lru_conv_w), at(lru_conv_b), at(gate_w_b), at(lru_gate_b), at(lru_lambda),
                            grp, total_rows, col["dx"], col["dy"]))
        outs = [jnp.concatenate(o, axis=0) for o in (oa, ob, oc, od)]
        merged = _merge(layer, row_mask, outs, w_branch_b, proj, col["gate"] * HEAD_DIM)
        h = _matmul(layer, merged, w_out_b, epilogue="residual", residual=h, out_dtype=F32)
        xn2 = _rmsnorm(h, at(norm_mlp_w), BF16)
        hid = _matmul(layer, xn2, w_up_b, epilogue="relu2")
        return _matmul(layer, hid, w_down_b, epilogue="residual", residual=h, out_dtype=F32)

    h = lax.fori_loop(0, depth, layer_fn, h)
    return tuple(_final_norm(h, final_norm_w, grp, x.dtype) for grp, x in zip(groups, xs))
```

```python
import functools
import math
from typing import NamedTuple

import numpy as np
import jax
import jax.numpy as jnp
from jax import lax
from jax.experimental import pallas as pl
from jax.experimental.pallas import tpu as pltpu

F32 = jnp.float32
BF16 = jnp.bfloat16

HEAD_DIM = 128
N_META = 16
BLK = 128
SUBLANES = 8
PAD_ROWS = BLK - N_META
WINDOW = 128
GRID_W = 64
ROPE_THETA = 10000.0
ROPE_FREQS = HEAD_DIM // 4
CONV_LEFT = 2
LRU_C = 8.0
NORM_EPS = 1e-6
GN_EPS = 1e-5
NEG_INF = -1e30
N_BRANCH = 4

V7X_VMEM_BYTES = 64 * 1024 * 1024
MM_TILE = 1024
MM_TILE_K = 4096
FLASH_TK = 256
FLASH_UNROLL = 8


class Group(NamedTuple):
    base: int
    batch: int
    nb: int
    m: int
    n_real: int

    @property
    def npad(self):
        return self.m * BLK + PAD_ROWS

    @property
    def rows(self):
        return self.nb * BLK


def _cparams(sem, vmem_mib):
    assert vmem_mib * 1024 * 1024 < V7X_VMEM_BYTES
    return pltpu.CompilerParams(dimension_semantics=sem, vmem_limit_bytes=vmem_mib * 1024 * 1024)


def _largest_tile(n, cap):
    t = cap
    while n % t:
        t //= 2
    return t


def _carry_rows(body, in_specs, args, prev):
    if prev is None:
        return body, in_specs, args, {}
    n_in = len(in_specs)

    def body_without_prev(*refs):
        return body(*refs[:n_in], *refs[n_in + 1:])

    return body_without_prev, in_specs + [pl.BlockSpec(memory_space=pl.ANY)], args + [prev], {n_in: 0}


def _rmsnorm_body(x_ref, w_ref, o_ref):
    x = x_ref[...].astype(F32)
    ms = jnp.mean(x * x, axis=-1, keepdims=True)
    o_ref[...] = ((x * lax.rsqrt(ms + NORM_EPS)) * w_ref[...]).astype(o_ref.dtype)


def _rmsnorm(h, w, out_dtype):
    t, d = h.shape
    tr = _largest_tile(t, 256)
    return pl.pallas_call(
        _rmsnorm_body,
        grid=(t // tr,),
        in_specs=[pl.BlockSpec((tr, d), lambda i: (i, 0)), pl.BlockSpec((1, d), lambda i: (0, 0))],
        out_specs=pl.BlockSpec((tr, d), lambda i: (i, 0)),
        out_shape=jax.ShapeDtypeStruct((t, d), out_dtype),
        compiler_params=_cparams(("parallel",), 32),
        name="rmsnorm",
    )(h, w.reshape(1, d).astype(F32))


def _final_norm(h, w, grp, out_dtype):
    _, d = h.shape
    nrb = grp.n_real // BLK
    return pl.pallas_call(
        _rmsnorm_body,
        grid=(grp.batch, nrb),
        in_specs=[pl.BlockSpec((BLK, d), lambda b, j: (grp.base + b * grp.nb + grp.m + 1 + j, 0)),
                  pl.BlockSpec((1, d), lambda b, j: (0, 0))],
        out_specs=pl.BlockSpec((None, BLK, d), lambda b, j: (b, j, 0)),
        out_shape=jax.ShapeDtypeStruct((grp.batch, grp.n_real, d), out_dtype),
        compiler_params=_cparams(("parallel", "parallel"), 32),
        name="final_norm",
    )(h, w.reshape(1, d).astype(F32))


def _mm_body(l_ref, a_ref, b_ref, *rest, epilogue, nk):
    del l_ref
    if epilogue == "residual":
        r_ref, o_ref, *scratch = rest
    else:
        o_ref, *scratch = rest

    def finish(acc):
        if epilogue == "relu2":
            acc = jnp.square(jnp.maximum(acc, 0.0))
        elif epilogue == "residual":
            acc = r_ref[...] + acc
        o_ref[...] = acc.astype(o_ref.dtype)

    prod = jnp.dot(a_ref[...], b_ref[...], preferred_element_type=F32)
    if nk == 1:
        finish(prod)
    elif epilogue == "residual" and o_ref.dtype == F32:
        k = pl.program_id(2)

        @pl.when(k == 0)
        def _():
            o_ref[...] = r_ref[...] + prod

        @pl.when(k > 0)
        def _():
            o_ref[...] += prod
    else:
        acc_ref, = scratch
        k = pl.program_id(2)

        @pl.when(k == 0)
        def _():
            acc_ref[...] = prod

        @pl.when(k > 0)
        def _():
            acc_ref[...] += prod

        @pl.when(k == nk - 1)
        def _():
            finish(acc_ref[...])


def _matmul(layer, a, w, *, epilogue="none", residual=None, out_dtype=BF16):
    t, kdim = a.shape
    n = w.shape[-1]
    tm = _largest_tile(t, MM_TILE)
    tn = _largest_tile(n, MM_TILE)
    tk = _largest_tile(kdim, MM_TILE_K)
    nk = kdim // tk
    out_accumulates = epilogue == "residual" and out_dtype == F32
    in_specs = [pl.BlockSpec((tm, tk), lambda i, j, k, l: (i, k)),
                pl.BlockSpec((None, tk, tn), lambda i, j, k, l: (l[0], k, j))]
    args = [a, w]
    aliases = {}
    if epilogue == "residual":
        in_specs.append(pl.BlockSpec((tm, tn), lambda i, j, k, l: (i, j)))
        args.append(residual)
        aliases = {3: 0}
    return pl.pallas_call(
        functools.partial(_mm_body, epilogue=epilogue, nk=nk),
        grid_spec=pltpu.PrefetchScalarGridSpec(
            num_scalar_prefetch=1,
            grid=(t // tm, n // tn, nk),
            in_specs=in_specs,
            out_specs=pl.BlockSpec((tm, tn), lambda i, j, k, l: (i, j)),
            scratch_shapes=[pltpu.VMEM((tm, tn), F32)] if nk > 1 and not out_accumulates else [],
        ),
        out_shape=jax.ShapeDtypeStruct((t, n), out_dtype),
        input_output_aliases=aliases,
        compiler_params=_cparams(("parallel", "parallel", "arbitrary"), 60),
        name="matmul_" + epilogue,
    )(layer, *args)


def _merge_body(l_ref, mask_ref, oa_ref, ob_ref, oc_ref, od_ref, wb_ref, ga_ref, gb_ref, gc_ref, gd_ref, out_ref):
    del l_ref
    acc = None
    for b, (o_ref, g_ref) in enumerate(((oa_ref, ga_ref), (ob_ref, gb_ref), (oc_ref, gc_ref), (od_ref, gd_ref))):
        proj = jnp.dot(o_ref[...], wb_ref[b], preferred_element_type=F32)
        term = proj * jax.nn.sigmoid(g_ref[...].astype(F32))
        acc = term if acc is None else acc + term
    out_ref[...] = jnp.where(mask_ref[...] > 0.0, acc, 0.0).astype(out_ref.dtype)


def _merge(layer, row_mask, outs, w_branch, proj, gate_col0):
    t, bw = outs[0].shape
    d = w_branch.shape[-1]
    tm = _largest_tile(t, MM_TILE)
    tn = _largest_tile(d, 512)
    assert gate_col0 % tn == 0 and d % tn == 0
    o_spec = pl.BlockSpec((tm, bw), lambda i, j, l: (i, 0))

    def gate_spec(b):
        return pl.BlockSpec((tm, tn), lambda i, j, l: (i, (gate_col0 + b * d) // tn + j))

    return pl.pallas_call(
        _merge_body,
        grid_spec=pltpu.PrefetchScalarGridSpec(
            num_scalar_prefetch=1,
            grid=(t // tm, d // tn),
            in_specs=[pl.BlockSpec((tm, 1), lambda i, j, l: (i, 0)), o_spec, o_spec, o_spec, o_spec,
                      pl.BlockSpec((None, N_BRANCH, bw, tn), lambda i, j, l: (l[0], 0, 0, j)),
                      gate_spec(0), gate_spec(1), gate_spec(2), gate_spec(3)],
            out_specs=pl.BlockSpec((tm, tn), lambda i, j, l: (i, j)),
        ),
        out_shape=jax.ShapeDtypeStruct((t, d), BF16),
        compiler_params=_cparams(("parallel", "arbitrary"), 56),
        name="branch_merge",
    )(layer, row_mask, *outs, w_branch, proj, proj, proj, proj)


def _attn_a_body(sink_ref, q_ref, km_ref, kp_ref, kc_ref, kn_ref, vm_ref, vp_ref, vc_ref, vn_ref, o_ref,
                 *, m, nb, heads, kv):
    j = pl.program_id(1)
    g = heads // kv
    scale = HEAD_DIM ** -0.5
    shape = (BLK, 4 * BLK)
    ri = lax.broadcasted_iota(jnp.int32, shape, 0)
    ci = lax.broadcasted_iota(jnp.int32, shape, 1)
    slot = lax.shift_right_logical(ci, 7)
    kin = lax.bitwise_and(ci, BLK - 1)
    dist = jnp.abs(ri - kin + (2 - slot) * BLK)
    kblk = j + slot - 2
    band_ok = (slot >= 1) & (kblk >= m + 1) & (kblk <= nb - 1) & (dist <= WINDOW)
    meta_ok = (slot == 0) & (kin >= PAD_ROWS)
    valid = band_ok | meta_ok
    distf = jnp.where(slot >= 1, dist, 0).astype(F32)
    for kh in range(kv):
        cs = slice(kh * HEAD_DIM, (kh + 1) * HEAD_DIM)
        kcat = jnp.concatenate([km_ref[:, cs], kp_ref[:, cs], kc_ref[:, cs], kn_ref[:, cs]], axis=0)
        vcat = jnp.concatenate([vm_ref[:, cs], vp_ref[:, cs], vc_ref[:, cs], vn_ref[:, cs]], axis=0)
        for gi in range(g):
            h = kh * g + gi
            hs = slice(h * HEAD_DIM, (h + 1) * HEAD_DIM)
            slope = 2.0 ** (-8.0 * (h + 1) / heads)
            s = lax.dot_general(q_ref[:, hs], kcat, (((1,), (1,)), ((), ())), preferred_element_type=F32) * scale
            s = jnp.where(valid, s - slope * distf, NEG_INF)
            sink = sink_ref[h]
            mx = jnp.maximum(jnp.max(s, axis=-1, keepdims=True), sink)
            p = jnp.exp(s - mx)
            den = jnp.sum(p, axis=-1, keepdims=True) + jnp.exp(sink - mx)
            p = (p / den).astype(BF16)
            o_ref[:, hs] = jnp.dot(p, vcat, preferred_element_type=F32).astype(o_ref.dtype)


def _attn_a(proj, sink, grp, heads, kv, col_q, col_k, col_v, prev):
    qw, kw = heads * HEAD_DIM, kv * HEAD_DIM
    assert col_q % heads == 0 and col_k % kv == 0 and col_v % kv == 0
    base, nb, m = grp.base, grp.nb, grp.m

    def kv_spec(col, which):
        def imap(b, j):
            if which == "meta":
                jj = m
            elif which == "prev":
                jj = jnp.maximum(j - 1, 0)
            elif which == "next":
                jj = jnp.minimum(j + 1, nb - 1)
            else:
                jj = j
            return (base + b * nb + jj, col // kv)
        return pl.BlockSpec((BLK, kw), imap)

    order = ("meta", "prev", "own", "next")
    in_specs = ([pl.BlockSpec(memory_space=pltpu.SMEM),
                 pl.BlockSpec((BLK, qw), lambda b, j: (base + b * nb + j, col_q // heads))]
                + [kv_spec(col_k, w) for w in order] + [kv_spec(col_v, w) for w in order])
    body = functools.partial(_attn_a_body, m=m, nb=nb, heads=heads, kv=kv)
    body, in_specs, args, aliases = _carry_rows(body, in_specs, [sink.astype(F32)] + [proj] * 9, prev)
    return pl.pallas_call(
        body,
        grid=(grp.batch, nb),
        in_specs=in_specs,
        out_specs=pl.BlockSpec((BLK, qw), lambda b, j: (base + b * nb + j, 0)),
        out_shape=jax.ShapeDtypeStruct((proj.shape[0], qw), BF16),
        input_output_aliases=aliases,
        compiler_params=_cparams(("parallel", "parallel"), 32),
        name="attn_window",
    )(*args)


def _rope_tables(grp):
    r = np.arange(grp.rows)
    t = r - grp.npad - N_META
    row = np.where(t >= 0, t // GRID_W, -1)
    col = np.where(t >= 0, t % GRID_W, np.clip(r - grp.npad, 0, N_META - 1))
    pos = jnp.asarray(np.stack([row, col], axis=-1), F32)
    inv_freq = ROPE_THETA ** (-jnp.arange(ROPE_FREQS, dtype=F32) / ROPE_FREQS)
    ang = pos[:, :, None] * inv_freq
    cos, sin = jnp.cos(ang), jnp.sin(ang)
    cos_t = jnp.concatenate([cos[:, 0], cos[:, 0], cos[:, 1], cos[:, 1]], axis=-1)
    sin_t = jnp.concatenate([-sin[:, 0], sin[:, 0], -sin[:, 1], sin[:, 1]], axis=-1)
    return cos_t, sin_t


def _prep_b_body(q_ref, k_ref, v_ref, cos_ref, sin_ref, gq_ref, gk_ref, qo_ref, ko_ref, vo_ref, *, heads, kv):
    cos, sin = cos_ref[...], sin_ref[...]
    lane = lax.broadcasted_iota(jnp.int32, (BLK, HEAD_DIM), 1)
    first_half = lax.bitwise_and(lane, 2 * ROPE_FREQS - 1) < ROPE_FREQS

    def norm_rope(x, w):
        x = x.astype(F32)
        ms = jnp.mean(x * x, axis=-1, keepdims=True)
        xn = (x * lax.rsqrt(ms + NORM_EPS)) * w
        partner = jnp.where(first_half, pltpu.roll(xn, HEAD_DIM - ROPE_FREQS, axis=1), pltpu.roll(xn, ROPE_FREQS, axis=1))
        return xn * cos + partner * sin

    for h in range(heads):
        hs = slice(h * HEAD_DIM, (h + 1) * HEAD_DIM)
        qo_ref[:, hs] = norm_rope(q_ref[:, hs], gq_ref[...]).astype(qo_ref.dtype)
    for h in range(kv):
        hs = slice(h * HEAD_DIM, (h + 1) * HEAD_DIM)
        ko_ref[h] = norm_rope(k_ref[:, hs], gk_ref[...]).T.astype(ko_ref.dtype)
    vo_ref[...] = v_ref[...]


def _prep_b(proj, gq, gk, tables, grp, heads, kv, col_q, col_k, col_v):
    qw, kw = heads * HEAD_DIM, kv * HEAD_DIM
    assert col_q % heads == 0 and col_k % kv == 0 and col_v % kv == 0
    base, nb = grp.base, grp.nb
    cos_t, sin_t = tables
    tab_spec = pl.BlockSpec((BLK, HEAD_DIM), lambda b, j: (j, 0))
    vec_spec = pl.BlockSpec((1, HEAD_DIM), lambda b, j: (0, 0))

    def in_spec(width, col, per):
        return pl.BlockSpec((BLK, width), lambda b, j: (base + b * nb + j, col // per))

    def out_spec(width):
        return pl.BlockSpec((None, BLK, width), lambda b, j: (b, j, 0))

    return pl.pallas_call(
        functools.partial(_prep_b_body, heads=heads, kv=kv),
        grid=(grp.batch, nb),
        in_specs=[in_spec(qw, col_q, heads), in_spec(kw, col_k, kv), in_spec(kw, col_v, kv),
                  tab_spec, tab_spec, vec_spec, vec_spec],
        out_specs=[out_spec(qw),
                   pl.BlockSpec((None, kv, None, HEAD_DIM, BLK), lambda b, j: (b, 0, j, 0, 0)),
                   out_spec(kw)],
        out_shape=[jax.ShapeDtypeStruct((grp.batch, grp.rows, qw), BF16),
                   jax.ShapeDtypeStruct((grp.batch, kv, nb, HEAD_DIM, BLK), BF16),
                   jax.ShapeDtypeStruct((grp.batch, grp.rows, kw), BF16)],
        compiler_params=_cparams(("parallel", "parallel"), 32),
        name="attn_axial_prep",
    )(proj, proj, proj, cos_t, sin_t, gq.reshape(1, HEAD_DIM).astype(F32), gk.reshape(1, HEAD_DIM).astype(F32))


def _flash_b_body(q_ref, k_ref, v_ref, o_ref, s_ref, m_ref, l_ref, acc_ref, *, m, n_real, kb, g):
    rows, tk = g * BLK, kb * BLK
    exp2_scale = HEAD_DIM ** -0.5 * math.log2(math.e)
    q = jnp.concatenate([q_ref[:, i * HEAD_DIM:(i + 1) * HEAD_DIM] for i in range(g)], axis=0)
    tiles = lambda x: [x[:, t * BLK:(t + 1) * BLK] for t in range(x.shape[1] // BLK)]

    meta_mask = lax.broadcasted_iota(jnp.int32, (rows, BLK), 1) >= PAD_ROWS
    s_meta = jnp.where(meta_mask, jnp.dot(q, k_ref[m], preferred_element_type=F32), NEG_INF)
    m_ref[...] = s_meta

    def phase1(c, carry):
        first = m + 1 + c * kb
        kc = jnp.concatenate([k_ref[first + t] for t in range(kb)], axis=1)
        s = jnp.dot(q, kc, preferred_element_type=F32)
        s_ref[c] = s
        m_ref[...] = jnp.maximum(m_ref[...], functools.reduce(jnp.maximum, tiles(s)))
        return carry

    lax.fori_loop(0, n_real // tk, phase1, 0, unroll=FLASH_UNROLL)
    mx = jnp.broadcast_to(jnp.max(m_ref[...], axis=-1, keepdims=True), (rows, BLK))
    m_ref[...] = mx
    p_meta = jnp.exp2((s_meta - mx) * exp2_scale)
    l_ref[...] = p_meta
    acc_ref[...] = jnp.dot(p_meta.astype(BF16), v_ref[m * BLK:(m + 1) * BLK, :], preferred_element_type=F32)

    def phase2(c, carry):
        mxc = m_ref[...]
        ps = [jnp.exp2((t - mxc) * exp2_scale) for t in tiles(s_ref[c])]
        l_ref[...] += functools.reduce(jnp.add, ps)
        p = jnp.concatenate([x.astype(BF16) for x in ps], axis=1)
        start = pl.multiple_of((m + 1 + c * kb) * BLK, BLK)
        acc_ref[...] += jnp.dot(p, v_ref[pl.ds(start, tk), :], preferred_element_type=F32)
        return carry

    lax.fori_loop(0, n_real // tk, phase2, 0, unroll=FLASH_UNROLL)
    out = acc_ref[...] / jnp.sum(l_ref[...], axis=-1, keepdims=True)
    o_ref[...] = jnp.concatenate([out[i * BLK:(i + 1) * BLK] for i in range(g)], axis=1).astype(o_ref.dtype)


def _flash_b(q, k, v, grp, heads, kv, total_rows, prev):
    g = heads // kv
    kb = FLASH_TK // BLK
    assert grp.n_real % FLASH_TK == 0
    rows = g * BLK
    base, nb = grp.base, grp.nb
    stat = pltpu.VMEM((rows, BLK), F32)
    in_specs = [pl.BlockSpec((None, BLK, g * HEAD_DIM), lambda b, kh, j: (b, j, kh)),
                pl.BlockSpec((None, None, nb, HEAD_DIM, BLK), lambda b, kh, j: (b, kh, 0, 0, 0)),
                pl.BlockSpec((None, grp.rows, HEAD_DIM), lambda b, kh, j: (b, 0, kh))]
    body = functools.partial(_flash_b_body, m=grp.m, n_real=grp.n_real, kb=kb, g=g)
    body, in_specs, args, aliases = _carry_rows(body, in_specs, [q, k, v], prev)
    return pl.pallas_call(
        body,
        grid=(grp.batch, kv, nb),
        in_specs=in_specs,
        out_specs=pl.BlockSpec((BLK, g * HEAD_DIM), lambda b, kh, j: (base + b * nb + j, kh)),
        out_shape=jax.ShapeDtypeStruct((total_rows, heads * HEAD_DIM), BF16),
        scratch_shapes=[pltpu.VMEM((grp.n_real // FLASH_TK, rows, FLASH_TK), F32), stat, stat, stat],
        input_output_aliases=aliases,
        compiler_params=_cparams(("parallel", "parallel", "arbitrary"), 48),
        name="attn_axial",
    )(*args)


def _ret_tables(l2d_ref, heads, tab_in, tab_q, tab_k, *, backward_only):
    ri = lax.broadcasted_iota(jnp.int32, (BLK, BLK), 0).astype(F32)
    ci = lax.broadcasted_iota(jnp.int32, (BLK, BLK), 1).astype(F32)
    for h in range(heads):
        lg_b = jnp.log1p(-jnp.exp2(l2d_ref[heads + h:heads + h + 1, :]))
        if backward_only:
            tab_q[h] = jnp.exp(lg_b * (BLK - ri))
            tab_k[h] = jnp.exp(lg_b * ri)
        else:
            lg_f = jnp.log1p(-jnp.exp2(l2d_ref[h:h + 1, :]))
            diff = ri - ci
            fwd = jnp.exp(lg_f * jnp.maximum(diff, 0.0))
            bwd = jnp.exp(lg_b * jnp.maximum(-diff, 0.0))
            tab_in[h] = jnp.where(diff >= 0, fwd, bwd)
            tab_q[h] = jnp.exp(lg_f * (ri + 1.0))
            tab_k[h] = jnp.exp(lg_f * (BLK - 1.0 - ri))


def _ret_bwd_body(l2d_ref, q_ref, k_ref, v_ref, y_ref, state, tab_q, tab_k, *, heads, nb, npad):
    b, j = pl.program_id(0), pl.program_id(1)
    jj = nb - 1 - j
    scale = HEAD_DIM ** -0.5

    @pl.when((b == 0) & (j == 0))
    def _():
        _ret_tables(l2d_ref, heads, None, tab_q, tab_k, backward_only=True)

    @pl.when(j == 0)
    def _():
        state[...] = jnp.zeros_like(state)

    valid = (jj * BLK + lax.broadcasted_iota(jnp.int32, (BLK, HEAD_DIM), 0)) >= npad
    for h in range(heads):
        hs = slice(h * HEAD_DIM, (h + 1) * HEAD_DIM)
        ks = jnp.where(valid, k_ref[:, hs].astype(F32) * scale, 0.0)
        vh = jnp.where(valid, v_ref[:, hs], jnp.zeros((), v_ref.dtype))
        st = state[h]
        y_ref[:, hs] = jnp.dot(q_ref[:, hs], st.astype(BF16), preferred_element_type=F32) * tab_q[h]
        kd = (ks * tab_k[h]).astype(BF16)
        upd = lax.dot_general(kd, vh, (((0,), (0,)), ((), ())), preferred_element_type=F32)
        state[h] = st * tab_q[h][0:1, :] + upd


def _ret_fwd_body(l2d_ref, q_ref, k_ref, v_ref, g_ref, yb_ref, gn_ref, o_ref, state, tab_in, tab_q, tab_k,
                  *, heads, npad):
    b, j = pl.program_id(0), pl.program_id(1)
    scale = HEAD_DIM ** -0.5

    @pl.when((b == 0) & (j == 0))
    def _():
        _ret_tables(l2d_ref, heads, tab_in, tab_q, tab_k, backward_only=False)

    @pl.when(j == 0)
    def _():
        state[...] = jnp.zeros_like(state)

    valid = (j * BLK + lax.broadcasted_iota(jnp.int32, (BLK, HEAD_DIM), 0)) >= npad
    for h in range(heads):
        hs = slice(h * HEAD_DIM, (h + 1) * HEAD_DIM)
        qh = q_ref[:, hs]
        ks = jnp.where(valid, k_ref[:, hs].astype(F32) * scale, 0.0)
        vh = jnp.where(valid, v_ref[:, hs], jnp.zeros((), v_ref.dtype))
        st = state[h]
        scores = lax.dot_general(qh, ks.astype(BF16), (((1,), (1,)), ((), ())), preferred_element_type=F32) * tab_in[h]
        inner = jnp.dot(scores.astype(BF16), vh, preferred_element_type=F32)
        cross = jnp.dot(qh, st.astype(BF16), preferred_element_type=F32) * tab_q[h]
        kd = (ks * tab_k[h]).astype(BF16)
        upd = lax.dot_general(kd, vh, (((0,), (0,)), ((), ())), preferred_element_type=F32)
        state[h] = st * tab_q[h][BLK - 1:BLK, :] + upd
        y = inner + cross + yb_ref[:, hs]
        mu = jnp.mean(y, axis=-1, keepdims=True)
        yc = y - mu
        var = jnp.mean(yc * yc, axis=-1, keepdims=True)
        yn = (yc * lax.rsqrt(var + GN_EPS)) * gn_ref[:, hs]
        o_ref[:, hs] = (jax.nn.silu(g_ref[:, hs].astype(F32)) * yn).astype(o_ref.dtype)


def _retention(proj, l2d, gn_w, grp, heads, col_q, col_k, col_v, col_g, prev):
    width = heads * HEAD_DIM
    assert all(c % heads == 0 for c in (col_q, col_k, col_v, col_g))
    base, nb = grp.base, grp.nb
    l2d_rows = jnp.broadcast_to(l2d.astype(F32).reshape(2 * heads, 1), (2 * heads, HEAD_DIM))
    l2d_spec = pl.BlockSpec((2 * heads, HEAD_DIM), lambda b, j: (0, 0))
    table = pltpu.VMEM((heads, BLK, BLK), F32)

    def in_spec(col, reverse):
        def imap(b, j):
            jj = nb - 1 - j if reverse else j
            return (base + b * nb + jj, col // heads)
        return pl.BlockSpec((BLK, width), imap)

    y_bwd = pl.pallas_call(
        functools.partial(_ret_bwd_body, heads=heads, nb=nb, npad=grp.npad),
        grid=(grp.batch, nb),
        in_specs=[l2d_spec, in_spec(col_q, True), in_spec(col_k, True), in_spec(col_v, True)],
        out_specs=pl.BlockSpec((BLK, width), lambda b, j: (b * nb + nb - 1 - j, 0)),
        out_shape=jax.ShapeDtypeStruct((grp.batch * grp.rows, width), F32),
        scratch_shapes=[table, table, table],
        compiler_params=_cparams(("arbitrary", "arbitrary"), 32),
        name="retention_bwd",
    )(l2d_rows, proj, proj, proj)
    in_specs = [l2d_spec, in_spec(col_q, False), in_spec(col_k, False), in_spec(col_v, False),
                in_spec(col_g, False), pl.BlockSpec((BLK, width), lambda b, j: (b * nb + j, 0)),
                pl.BlockSpec((1, width), lambda b, j: (0, 0))]
    body = functools.partial(_ret_fwd_body, heads=heads, npad=grp.npad)
    args = [l2d_rows, proj, proj, proj, proj, y_bwd, gn_w.reshape(1, width).astype(F32)]
    body, in_specs, args, aliases = _carry_rows(body, in_specs, args, prev)
    return pl.pallas_call(
        body,
        grid=(grp.batch, nb),
        in_specs=in_specs,
        out_specs=pl.BlockSpec((BLK, width), lambda b, j: (base + b * nb + j, 0)),
        out_shape=jax.ShapeDtypeStruct((proj.shape[0], width), BF16),
        scratch_shapes=[table, table, table, table],
        input_output_aliases=aliases,
        compiler_params=_cparams(("arbitrary", "arbitrary"), 32),
        name="retention_fwd",
    )(*args)


def _softplus(x):
    return jnp.maximum(x, 0.0) + jnp.log1p(jnp.exp(-jnp.abs(x)))


def _lru_block(x_ref, prev_ref, next_ref, cw_ref, cb_ref, gw_ref, gb_ref, lam_ref, carry_ref, jj, *,
               nblocks, npad, seq_rows, reverse):
    width = x_ref.shape[-1]
    tail = prev_ref.shape[0]
    ri = lax.broadcasted_iota(jnp.int32, (BLK, width), 0)
    pos = jj * BLK + ri
    valid = pos >= npad
    x = jnp.where(valid, x_ref[...].astype(F32), 0.0)
    pi = jj * BLK - tail + lax.broadcasted_iota(jnp.int32, (tail, width), 0)
    prev = jnp.where(pi >= npad, prev_ref[...].astype(F32), 0.0)
    ni = (jj + 1) * BLK + lax.broadcasted_iota(jnp.int32, (tail, width), 0)
    nxt = jnp.where(ni < seq_rows, next_ref[...].astype(F32), 0.0)
    p1 = prev[tail - 1:tail, :]
    p2 = prev[tail - 2:tail - 1, :]
    n0 = nxt[0:1, :]
    x_m1 = jnp.where(ri == 0, p1, pltpu.roll(x, 1, axis=0))
    x_m2 = jnp.where(ri == 0, p2, jnp.where(ri == 1, p1, pltpu.roll(x, 2, axis=0)))
    x_p1 = jnp.where(ri == BLK - 1, n0, pltpu.roll(x, BLK - 1, axis=0))
    xc = (cw_ref[0:1, :] * x_m2 + cw_ref[1:2, :] * x_m1 + cw_ref[2:3, :] * x + cw_ref[3:4, :] * x_p1) + cb_ref[...]
    xcb = xc.astype(BF16)
    rl = lax.broadcasted_iota(jnp.int32, (BLK, HEAD_DIM), 0)
    sub = lax.bitwise_and(rl, SUBLANES - 1)
    keeps = [(sh, (sub < SUBLANES - sh) if reverse else (sub >= sh)) for sh in (1, 2, 4)]
    ok = (jj * BLK + rl) >= npad
    outs = []
    for c in range(nblocks):
        cs = slice(c * HEAD_DIM, (c + 1) * HEAD_DIM)
        gr = jnp.dot(xcb[:, cs], gw_ref[0, c], preferred_element_type=F32) + gb_ref[0:1, cs]
        gi = jnp.dot(xcb[:, cs], gw_ref[1, c], preferred_element_type=F32) + gb_ref[1:2, cs]
        r = jax.nn.sigmoid(gr)
        i = jax.nn.sigmoid(gi)
        log_a = (-LRU_C * r) * _softplus(-lam_ref[:, cs])
        ea = jnp.exp(log_a)
        u = jnp.where(ok, jnp.sqrt(1.0 - ea * ea) * (i * xc[:, cs]), 0.0)
        a = jnp.where(ok, ea, 1.0)
        for sh, keep in keeps:
            shift = BLK - sh if reverse else sh
            a_s = pltpu.roll(a, shift, axis=0)
            u_s = pltpu.roll(u, shift, axis=0)
            u = jnp.where(keep, a * u_s + u, u)
            a = jnp.where(keep, a * a_s, a)
        carry = carry_ref[:, cs]
        groups = [None] * (BLK // SUBLANES)
        edge = 0 if reverse else SUBLANES - 1
        for v in (reversed(range(len(groups))) if reverse else range(len(groups))):
            rows = slice(v * SUBLANES, (v + 1) * SUBLANES)
            hv = u[rows] + a[rows] * carry
            carry = hv[edge:edge + 1, :]
            groups[v] = hv
        carry_ref[:, cs] = carry
        outs.append(jnp.concatenate(groups, axis=0))
    return outs


def _lru_bwd_body(x_ref, prev_ref, next_ref, cw_ref, cb_ref, gw_ref, gb_ref, lam_ref, h_ref, carry_ref,
                  *, nb, nblocks, npad):
    j = pl.program_id(1)

    @pl.when(j == 0)
    def _():
        carry_ref[...] = jnp.zeros_like(carry_ref)

    outs = _lru_block(x_ref, prev_ref, next_ref, cw_ref, cb_ref, gw_ref, gb_ref, lam_ref, carry_ref, nb - 1 - j,
                      nblocks=nblocks, npad=npad, seq_rows=nb * BLK, reverse=True)
    for c, hcol in enumerate(outs):
        h_ref[:, c * HEAD_DIM:(c + 1) * HEAD_DIM] = hcol


def _lru_fwd_body(x_ref, prev_ref, next_ref, y_ref, hb_ref, cw_ref, cb_ref, gw_ref, gb_ref, lam_ref, o_ref, carry_ref,
                  *, nb, nblocks, npad):
    j = pl.program_id(1)

    @pl.when(j == 0)
    def _():
        carry_ref[...] = jnp.zeros_like(carry_ref)

    outs = _lru_block(x_ref, prev_ref, next_ref, cw_ref, cb_ref, gw_ref, gb_ref, lam_ref, carry_ref, j,
                      nblocks=nblocks, npad=npad, seq_rows=nb * BLK, reverse=False)
    for c, hcol in enumerate(outs):
        cs = slice(c * HEAD_DIM, (c + 1) * HEAD_DIM)
        gate = jax.nn.gelu(y_ref[:, cs].astype(F32))
        o_ref[:, cs] = ((hcol + hb_ref[:, cs]) * gate).astype(o_ref.dtype)


def _hawk(proj, conv_w, conv_b, gate_w, gate_b, lam, grp, total_rows, col_x, col_y, prev):
    nblocks = gate_w.shape[2]
    width = nblocks * HEAD_DIM
    tail = 16
    per = BLK // tail
    assert col_x % nblocks == 0 and col_y % nblocks == 0
    base, nb = grp.base, grp.nb
    last_tail = total_rows // tail - 1

    def specs(reverse):
        def blk(b, j):
            return base + b * nb + (nb - 1 - j if reverse else j)
        return [pl.BlockSpec((BLK, width), lambda b, j: (blk(b, j), col_x // nblocks)),
                pl.BlockSpec((tail, width), lambda b, j: (jnp.maximum(blk(b, j) * per - 1, 0), col_x // nblocks)),
                pl.BlockSpec((tail, width), lambda b, j: (jnp.minimum((blk(b, j) + 1) * per, last_tail), col_x // nblocks))]

    def param_specs(d):
        return [pl.BlockSpec((conv_w.shape[0], width), lambda b, j: (0, 0)),
                pl.BlockSpec((1, width), lambda b, j: (0, 0)),
                pl.BlockSpec((None, 2, nblocks, HEAD_DIM, HEAD_DIM), lambda b, j: (d, 0, 0, 0, 0)),
                pl.BlockSpec((None, 2, width), lambda b, j: (d, 0, 0)),
                pl.BlockSpec((None, 1, width), lambda b, j: (d, 0, 0))]

    params = (conv_w.astype(F32), conv_b.reshape(1, width).astype(F32), gate_w, gate_b.astype(F32),
              lam.reshape(2, 1, width).astype(F32))
    carry = pltpu.VMEM((1, width), F32)
    h_bwd = pl.pallas_call(
        functools.partial(_lru_bwd_body, nb=nb, nblocks=nblocks, npad=grp.npad),
        grid=(grp.batch, nb),
        in_specs=specs(True) + param_specs(1),
        out_specs=pl.BlockSpec((BLK, width), lambda b, j: (b * nb + nb - 1 - j, 0)),
        out_shape=jax.ShapeDtypeStruct((grp.batch * grp.rows, width), F32),
        scratch_shapes=[carry],
        compiler_params=_cparams(("arbitrary", "arbitrary"), 32),
        name="rglru_bwd",
    )(proj, proj, proj, *params)
    in_specs = (specs(False)
                + [pl.BlockSpec((BLK, width), lambda b, j: (base + b * nb + j, col_y // nblocks)),
                   pl.BlockSpec((BLK, width), lambda b, j: (b * nb + j, 0))]
                + param_specs(0))
    body = functools.partial(_lru_fwd_body, nb=nb, nblocks=nblocks, npad=grp.npad)
    body, in_specs, args, aliases = _carry_rows(body, in_specs, [proj, proj, proj, proj, h_bwd, *params], prev)
    return pl.pallas_call(
        body,
        grid=(grp.batch, nb),
        in_specs=in_specs,
        out_specs=pl.BlockSpec((BLK, width), lambda b, j: (base + b * nb + j, 0)),
        out_shape=jax.ShapeDtypeStruct((total_rows, width), BF16),
        scratch_shapes=[carry],
        input_output_aliases=aliases,
        compiler_params=_cparams(("arbitrary", "arbitrary"), 32),
        name="rglru_fwd",
    )(*args)


def _plan(batches, seqs):
    per_tile = MM_TILE // BLK
    nbs = [1 + s // BLK for s in seqs]
    total = sum(b * nb for b, nb in zip(batches, nbs))
    extra = (-total) % per_tile
    m0 = extra // batches[0] if extra % batches[0] == 0 else 0
    groups, base = [], 0
    for gi, (b, s, nb) in enumerate(zip(batches, seqs, nbs)):
        m = m0 if gi == 0 else 0
        groups.append(Group(base=base, batch=b, nb=nb + m, m=m, n_real=s))
        base += b * (nb + m)
    return groups, base * BLK


def kernel(x_prompt, x_sample, meta_tokens, norm_mix_w, w_in, attn_sink, qk_norm_q, qk_norm_k, ret_log2_decay,
           ret_gn_w, lru_conv_w, lru_conv_b, lru_gate_w, lru_gate_b, lru_lambda, w_branch, w_out, norm_mlp_w,
           w_up, w_down, final_norm_w):
    xs = (x_prompt, x_sample)
    d = x_prompt.shape[-1]
    depth = w_in.shape[0]
    bw = w_branch.shape[2]
    heads = bw // HEAD_DIM
    kv = heads // 4
    assert all(x.shape[1] % BLK == 0 and x.shape[1] % GRID_W == 0 for x in xs) and bw % HEAD_DIM == 0
    groups, total_rows = _plan([x.shape[0] for x in xs], [x.shape[1] for x in xs])

    parts, mask_parts = [], []
    for grp, x in zip(groups, xs):
        pad = jnp.zeros((grp.batch, grp.npad, d), x.dtype)
        meta = jnp.broadcast_to(meta_tokens.astype(x.dtype)[None], (grp.batch, N_META, d))
        parts.append(jnp.concatenate([pad, meta, x], axis=1).reshape(grp.batch * grp.rows, d))
        mask_parts.append(np.tile(np.arange(grp.rows) >= grp.npad, grp.batch))
    h = jnp.concatenate(parts, axis=0).astype(F32)
    row_mask = jnp.asarray(np.concatenate(mask_parts).astype(np.float32).reshape(total_rows, 1))

    hw, kw = heads * HEAD_DIM, kv * HEAD_DIM
    widths = (hw, kw, kw, hw, kw, kw, bw, bw, bw, bw, bw, bw, N_BRANCH * d)
    assert sum(widths) == w_in.shape[-1]
    offs = np.concatenate([[0], np.cumsum(widths)])
    cols = [w_in[:, :, int(offs[i]):int(offs[i + 1])] for i in range(len(widths))]
    order = (0, 3, 1, 2, 4, 5, 6, 7, 8, 9, 10, 11, 12)
    w_in_b = jnp.concatenate([cols[i] for i in order], axis=-1).astype(BF16)
    col, names = {}, ("aq", "bq", "ak", "av", "bk", "bv", "cq", "ck", "cv", "cg", "dx", "dy", "gate")
    acc = 0
    for name, i in zip(names, order):
        col[name] = acc // HEAD_DIM
        acc += widths[i]
    w_branch_b, w_out_b, w_up_b, w_down_b = (w.astype(BF16) for w in (w_branch, w_out, w_up, w_down))
    gate_w_b = lru_gate_w.astype(BF16)
    rope_tables = [_rope_tables(grp) for grp in groups]

    def layer_fn(l, h):
        layer = jnp.full((1,), l, jnp.int32)
        at = lambda p: lax.dynamic_index_in_dim(p, l, axis=0, keepdims=False)
        xn = _rmsnorm(h, at(norm_mix_w), BF16)
        proj = _matmul(layer, xn, w_in_b)
        oa = ob = oc = od = None
        for grp, tables in zip(groups, rope_tables):
            oa = _attn_a(proj, at(attn_sink), grp, heads, kv, col["aq"], col["ak"], col["av"], oa)
            qb, kb, vb = _prep_b(proj, at(qk_norm_q), at(qk_norm_k), tables, grp, heads, kv,
                                 col["bq"], col["bk"], col["bv"])
            ob = _flash_b(qb, kb, vb, grp, heads, kv, total_rows, ob)
            oc = _retention(proj, at(ret_log2_decay), at(ret_gn_w), grp, heads,
                            col["cq"], col["ck"], col["cv"], col["cg"], oc)
            od = _hawk(proj, at(lru_conv_w), at(lru_conv_b), at(gate_w_b), at(lru_gate_b), at(lru_lambda),
                       grp, total_rows, col["dx"], col["dy"], od)
        merged = _merge(layer, row_mask, (oa, ob, oc, od), w_branch_b, proj, col["gate"] * HEAD_DIM)
        h = _matmul(layer, merged, w_out_b, epilogue="residual", residual=h, out_dtype=F32)
        xn2 = _rmsnorm(h, at(norm_mlp_w), BF16)
        hid = _matmul(layer, xn2, w_up_b, epilogue="relu2")
        return _matmul(layer, hid, w_down_b, epilogue="residual", residual=h, out_dtype=F32)

    h = lax.fori_loop(0, depth, layer_fn, h)
    return tuple(_final_norm(h, final_norm_w, grp, x.dtype) for grp, x in zip(groups, xs))
```

```python
import functools
import math
from typing import NamedTuple

import numpy as np
import jax
import jax.numpy as jnp
from jax import lax
from jax.experimental import pallas as pl
from jax.experimental.pallas import tpu as pltpu

F32 = jnp.float32
BF16 = jnp.bfloat16

HEAD_DIM = 128
N_META = 16
BLK = 128
SUBLANES = 8
PAD_ROWS = BLK - N_META
WINDOW = 128
GRID_W = 64
ROPE_THETA = 10000.0
ROPE_FREQS = HEAD_DIM // 4
CONV_LEFT = 2
LRU_C = 8.0
NORM_EPS = 1e-6
GN_EPS = 1e-5
NEG_INF = -1e30
N_BRANCH = 4

V7X_VMEM_BYTES = 64 * 1024 * 1024
MM_TILE = 1024
MM_TILE_K = 4096
FLASH_TK = 256
FLASH_UNROLL = 8


class Group(NamedTuple):
    base: int
    batch: int
    nb: int
    m: int
    n_real: int

    @property
    def npad(self):
        return self.m * BLK + PAD_ROWS

    @property
    def rows(self):
        return self.nb * BLK


def _cparams(sem, vmem_mib):
    assert vmem_mib * 1024 * 1024 < V7X_VMEM_BYTES
    return pltpu.CompilerParams(dimension_semantics=sem, vmem_limit_bytes=vmem_mib * 1024 * 1024)


def _largest_tile(n, cap):
    t = cap
    while n % t:
        t //= 2
    return t


def _carry_rows(body, in_specs, args, prev):
    if prev is None:
        return body, in_specs, args, {}
    n_in = len(in_specs)

    def body_without_prev(*refs):
        return body(*refs[:n_in], *refs[n_in + 1:])

    return body_without_prev, in_specs + [pl.BlockSpec(memory_space=pl.ANY)], args + [prev], {n_in: 0}


def _embed_body(x_ref, meta_ref, o_ref, *, m):
    j = pl.program_id(1)

    @pl.when(j > m)
    def _():
        o_ref[...] = x_ref[...].astype(o_ref.dtype)

    @pl.when(j == m)
    def _():
        o_ref[0:PAD_ROWS, :] = jnp.zeros((PAD_ROWS, o_ref.shape[1]), o_ref.dtype)
        o_ref[PAD_ROWS:BLK, :] = meta_ref[...].astype(o_ref.dtype)

    @pl.when(j < m)
    def _():
        o_ref[...] = jnp.zeros_like(o_ref)


def _embed(x, meta, grp, total_rows, prev):
    d = x.shape[-1]
    base, nb, m = grp.base, grp.nb, grp.m
    in_specs = [pl.BlockSpec((None, BLK, d), lambda b, j: (b, jnp.maximum(j - m - 1, 0), 0)),
                pl.BlockSpec((N_META, d), lambda b, j: (0, 0))]
    body, in_specs, args, aliases = _carry_rows(functools.partial(_embed_body, m=m), in_specs, [x, meta], prev)
    return pl.pallas_call(
        body,
        grid=(grp.batch, nb),
        in_specs=in_specs,
        out_specs=pl.BlockSpec((BLK, d), lambda b, j: (base + b * nb + j, 0)),
        out_shape=jax.ShapeDtypeStruct((total_rows, d), F32),
        input_output_aliases=aliases,
        compiler_params=_cparams(("parallel", "arbitrary"), 32),
        name="embed_tokens",
    )(*args)


def _rmsnorm_body(x_ref, w_ref, o_ref):
    x = x_ref[...].astype(F32)
    ms = jnp.mean(x * x, axis=-1, keepdims=True)
    o_ref[...] = ((x * lax.rsqrt(ms + NORM_EPS)) * w_ref[...]).astype(o_ref.dtype)


def _rmsnorm(h, w, out_dtype):
    t, d = h.shape
    tr = _largest_tile(t, 256)
    return pl.pallas_call(
        _rmsnorm_body,
        grid=(t // tr,),
        in_specs=[pl.BlockSpec((tr, d), lambda i: (i, 0)), pl.BlockSpec((1, d), lambda i: (0, 0))],
        out_specs=pl.BlockSpec((tr, d), lambda i: (i, 0)),
        out_shape=jax.ShapeDtypeStruct((t, d), out_dtype),
        compiler_params=_cparams(("parallel",), 32),
        name="rmsnorm",
    )(h, w.reshape(1, d).astype(F32))


def _final_norm(h, w, grp, out_dtype):
    _, d = h.shape
    nrb = grp.n_real // BLK
    return pl.pallas_call(
        _rmsnorm_body,
        grid=(grp.batch, nrb),
        in_specs=[pl.BlockSpec((BLK, d), lambda b, j: (grp.base + b * grp.nb + grp.m + 1 + j, 0)),
                  pl.BlockSpec((1, d), lambda b, j: (0, 0))],
        out_specs=pl.BlockSpec((None, BLK, d), lambda b, j: (b, j, 0)),
        out_shape=jax.ShapeDtypeStruct((grp.batch, grp.n_real, d), out_dtype),
        compiler_params=_cparams(("parallel", "parallel"), 32),
        name="final_norm",
    )(h, w.reshape(1, d).astype(F32))


def _mm_body(l_ref, a_ref, b_ref, *rest, epilogue, nk):
    del l_ref
    if epilogue == "residual":
        r_ref, o_ref, *scratch = rest
    else:
        o_ref, *scratch = rest

    def finish(acc):
        if epilogue == "relu2":
            acc = jnp.square(jnp.maximum(acc, 0.0))
        elif epilogue == "residual":
            acc = r_ref[...] + acc
        o_ref[...] = acc.astype(o_ref.dtype)

    prod = jnp.dot(a_ref[...], b_ref[...], preferred_element_type=F32)
    if nk == 1:
        finish(prod)
    elif epilogue == "residual" and o_ref.dtype == F32:
        k = pl.program_id(2)

        @pl.when(k == 0)
        def _():
            o_ref[...] = r_ref[...] + prod

        @pl.when(k > 0)
        def _():
            o_ref[...] += prod
    else:
        acc_ref, = scratch
        k = pl.program_id(2)

        @pl.when(k == 0)
        def _():
            acc_ref[...] = prod

        @pl.when(k > 0)
        def _():
            acc_ref[...] += prod

        @pl.when(k == nk - 1)
        def _():
            finish(acc_ref[...])


def _matmul(layer, a, w, *, epilogue="none", residual=None, out_dtype=BF16):
    t, kdim = a.shape
    n = w.shape[-1]
    tm = _largest_tile(t, MM_TILE)
    tn = _largest_tile(n, MM_TILE)
    tk = _largest_tile(kdim, MM_TILE_K)
    nk = kdim // tk
    out_accumulates = epilogue == "residual" and out_dtype == F32
    in_specs = [pl.BlockSpec((tm, tk), lambda i, j, k, l: (i, k)),
                pl.BlockSpec((None, tk, tn), lambda i, j, k, l: (l[0], k, j))]
    args = [a, w]
    aliases = {}
    if epilogue == "residual":
        in_specs.append(pl.BlockSpec((tm, tn), lambda i, j, k, l: (i, j)))
        args.append(residual)
        aliases = {3: 0}
    return pl.pallas_call(
        functools.partial(_mm_body, epilogue=epilogue, nk=nk),
        grid_spec=pltpu.PrefetchScalarGridSpec(
            num_scalar_prefetch=1,
            grid=(t // tm, n // tn, nk),
            in_specs=in_specs,
            out_specs=pl.BlockSpec((tm, tn), lambda i, j, k, l: (i, j)),
            scratch_shapes=[pltpu.VMEM((tm, tn), F32)] if nk > 1 and not out_accumulates else [],
        ),
        out_shape=jax.ShapeDtypeStruct((t, n), out_dtype),
        input_output_aliases=aliases,
        compiler_params=_cparams(("parallel", "parallel", "arbitrary"), 60),
        name="matmul_" + epilogue,
    )(layer, *args)


def _merge_body(l_ref, mask_ref, oa_ref, ob_ref, oc_ref, od_ref, wb_ref, ga_ref, gb_ref, gc_ref, gd_ref, out_ref):
    del l_ref
    acc = None
    for b, (o_ref, g_ref) in enumerate(((oa_ref, ga_ref), (ob_ref, gb_ref), (oc_ref, gc_ref), (od_ref, gd_ref))):
        proj = jnp.dot(o_ref[...], wb_ref[b], preferred_element_type=F32)
        term = proj * jax.nn.sigmoid(g_ref[...].astype(F32))
        acc = term if acc is None else acc + term
    out_ref[...] = jnp.where(mask_ref[...] > 0.0, acc, 0.0).astype(out_ref.dtype)


def _merge(layer, row_mask, outs, w_branch, proj, gate_col0):
    t, bw = outs[0].shape
    d = w_branch.shape[-1]
    tm = _largest_tile(t, MM_TILE)
    tn = _largest_tile(d, 512)
    assert gate_col0 % tn == 0 and d % tn == 0
    o_spec = pl.BlockSpec((tm, bw), lambda i, j, l: (i, 0))

    def gate_spec(b):
        return pl.BlockSpec((tm, tn), lambda i, j, l: (i, (gate_col0 + b * d) // tn + j))

    return pl.pallas_call(
        _merge_body,
        grid_spec=pltpu.PrefetchScalarGridSpec(
            num_scalar_prefetch=1,
            grid=(t // tm, d // tn),
            in_specs=[pl.BlockSpec((tm, 1), lambda i, j, l: (i, 0)), o_spec, o_spec, o_spec, o_spec,
                      pl.BlockSpec((None, N_BRANCH, bw, tn), lambda i, j, l: (l[0], 0, 0, j)),
                      gate_spec(0), gate_spec(1), gate_spec(2), gate_spec(3)],
            out_specs=pl.BlockSpec((tm, tn), lambda i, j, l: (i, j)),
        ),
        out_shape=jax.ShapeDtypeStruct((t, d), BF16),
        compiler_params=_cparams(("parallel", "arbitrary"), 56),
        name="branch_merge",
    )(layer, row_mask, *outs, w_branch, proj, proj, proj, proj)


def _attn_a_body(sink_ref, q_ref, km_ref, kp_ref, kc_ref, kn_ref, vm_ref, vp_ref, vc_ref, vn_ref, o_ref,
                 *, m, nb, heads, kv):
    j = pl.program_id(1)
    g = heads // kv
    scale = HEAD_DIM ** -0.5
    shape = (BLK, 4 * BLK)
    ri = lax.broadcasted_iota(jnp.int32, shape, 0)
    ci = lax.broadcasted_iota(jnp.int32, shape, 1)
    slot = lax.shift_right_logical(ci, 7)
    kin = lax.bitwise_and(ci, BLK - 1)
    dist = jnp.abs(ri - kin + (2 - slot) * BLK)
    kblk = j + slot - 2
    band_ok = (slot >= 1) & (kblk >= m + 1) & (kblk <= nb - 1) & (dist <= WINDOW)
    meta_ok = (slot == 0) & (kin >= PAD_ROWS)
    valid = band_ok | meta_ok
    distf = jnp.where(slot >= 1, dist, 0).astype(F32)
    for kh in range(kv):
        cs = slice(kh * HEAD_DIM, (kh + 1) * HEAD_DIM)
        kcat = jnp.concatenate([km_ref[:, cs], kp_ref[:, cs], kc_ref[:, cs], kn_ref[:, cs]], axis=0)
        vcat = jnp.concatenate([vm_ref[:, cs], vp_ref[:, cs], vc_ref[:, cs], vn_ref[:, cs]], axis=0)
        hss = [slice((kh * g + gi) * HEAD_DIM, (kh * g + gi + 1) * HEAD_DIM) for gi in range(g)]
        q4 = jnp.concatenate([q_ref[:, hs] for hs in hss], axis=0)
        s4 = lax.dot_general(q4, kcat, (((1,), (1,)), ((), ())), preferred_element_type=F32)
        ps = []
        for gi in range(g):
            h = kh * g + gi
            slope = 2.0 ** (-8.0 * (h + 1) / heads)
            s = jnp.where(valid, s4[gi * BLK:(gi + 1) * BLK] * scale - slope * distf, NEG_INF)
            sink = sink_ref[h]
            mx = jnp.maximum(jnp.max(s, axis=-1, keepdims=True), sink)
            p = jnp.exp(s - mx)
            den = jnp.sum(p, axis=-1, keepdims=True) + jnp.exp(sink - mx)
            ps.append((p / den).astype(BF16))
        o4 = jnp.dot(jnp.concatenate(ps, axis=0), vcat, preferred_element_type=F32)
        for gi, hs in enumerate(hss):
            o_ref[:, hs] = o4[gi * BLK:(gi + 1) * BLK].astype(o_ref.dtype)


def _attn_a(proj, sink, grp, heads, kv, col_q, col_k, col_v, prev):
    qw, kw = heads * HEAD_DIM, kv * HEAD_DIM
    assert col_q % heads == 0 and col_k % kv == 0 and col_v % kv == 0
    base, nb, m = grp.base, grp.nb, grp.m

    def kv_spec(col, which):
        def imap(b, j):
            if which == "meta":
                jj = m
            elif which == "prev":
                jj = jnp.maximum(j - 1, 0)
            elif which == "next":
                jj = jnp.minimum(j + 1, nb - 1)
            else:
                jj = j
            return (base + b * nb + jj, col // kv)
        return pl.BlockSpec((BLK, kw), imap)

    order = ("meta", "prev", "own", "next")
    in_specs = ([pl.BlockSpec(memory_space=pltpu.SMEM),
                 pl.BlockSpec((BLK, qw), lambda b, j: (base + b * nb + j, col_q // heads))]
                + [kv_spec(col_k, w) for w in order] + [kv_spec(col_v, w) for w in order])
    body = functools.partial(_attn_a_body, m=m, nb=nb, heads=heads, kv=kv)
    body, in_specs, args, aliases = _carry_rows(body, in_specs, [sink.astype(F32)] + [proj] * 9, prev)
    return pl.pallas_call(
        body,
        grid=(grp.batch, nb),
        in_specs=in_specs,
        out_specs=pl.BlockSpec((BLK, qw), lambda b, j: (base + b * nb + j, 0)),
        out_shape=jax.ShapeDtypeStruct((proj.shape[0], qw), BF16),
        input_output_aliases=aliases,
        compiler_params=_cparams(("parallel", "parallel"), 32),
        name="attn_window",
    )(*args)


def _rope_tables(grp):
    r = np.arange(grp.rows)
    t = r - grp.npad - N_META
    row = np.where(t >= 0, t // GRID_W, -1)
    col = np.where(t >= 0, t % GRID_W, np.clip(r - grp.npad, 0, N_META - 1))
    pos = jnp.asarray(np.stack([row, col], axis=-1), F32)
    inv_freq = ROPE_THETA ** (-jnp.arange(ROPE_FREQS, dtype=F32) / ROPE_FREQS)
    ang = pos[:, :, None] * inv_freq
    cos, sin = jnp.cos(ang), jnp.sin(ang)
    cos_t = jnp.concatenate([cos[:, 0], cos[:, 0], cos[:, 1], cos[:, 1]], axis=-1)
    sin_t = jnp.concatenate([-sin[:, 0], sin[:, 0], -sin[:, 1], sin[:, 1]], axis=-1)
    return cos_t, sin_t


def _prep_b_body(qa_ref, qb_ref, k_ref, v_ref, cos_ref, sin_ref, gq_ref, gk_ref, qo_ref, ko_ref, vo_ref, *, heads, kv):
    cos, sin = cos_ref[...], sin_ref[...]
    lane = lax.broadcasted_iota(jnp.int32, (BLK, HEAD_DIM), 1)
    first_half = lax.bitwise_and(lane, 2 * ROPE_FREQS - 1) < ROPE_FREQS

    def norm_rope(x, w):
        x = x.astype(F32)
        ms = jnp.mean(x * x, axis=-1, keepdims=True)
        xn = (x * lax.rsqrt(ms + NORM_EPS)) * w
        partner = jnp.where(first_half, pltpu.roll(xn, HEAD_DIM - ROPE_FREQS, axis=1), pltpu.roll(xn, ROPE_FREQS, axis=1))
        return xn * cos + partner * sin

    half = heads // 2
    for h in range(heads):
        src = qa_ref if h < half else qb_ref
        hl = h % half
        qo_ref[:, h * HEAD_DIM:(h + 1) * HEAD_DIM] = norm_rope(
            src[:, hl * HEAD_DIM:(hl + 1) * HEAD_DIM], gq_ref[...]).astype(qo_ref.dtype)
    for h in range(kv):
        hs = slice(h * HEAD_DIM, (h + 1) * HEAD_DIM)
        ko_ref[h] = norm_rope(k_ref[:, hs], gk_ref[...]).T.astype(ko_ref.dtype)
    vo_ref[...] = v_ref[...]


def _prep_b(proj, gq, gk, tables, grp, heads, kv, col_q, col_k, col_v):
    qw, kw = heads * HEAD_DIM, kv * HEAD_DIM
    half = heads // 2
    assert heads % 2 == 0 and col_q % half == 0 and col_k % kv == 0 and col_v % kv == 0
    base, nb = grp.base, grp.nb
    cos_t, sin_t = tables
    tab_spec = pl.BlockSpec((BLK, HEAD_DIM), lambda b, j: (j, 0))
    vec_spec = pl.BlockSpec((1, HEAD_DIM), lambda b, j: (0, 0))

    def in_spec(width, col, per):
        return pl.BlockSpec((BLK, width), lambda b, j: (base + b * nb + j, col // per))

    def out_spec(width):
        return pl.BlockSpec((None, BLK, width), lambda b, j: (b, j, 0))

    return pl.pallas_call(
        functools.partial(_prep_b_body, heads=heads, kv=kv),
        grid=(grp.batch, nb),
        in_specs=[in_spec(qw // 2, col_q, half), in_spec(qw // 2, col_q + half, half),
                  in_spec(kw, col_k, kv), in_spec(kw, col_v, kv), tab_spec, tab_spec, vec_spec, vec_spec],
        out_specs=[out_spec(qw),
                   pl.BlockSpec((None, kv, None, HEAD_DIM, BLK), lambda b, j: (b, 0, j, 0, 0)),
                   out_spec(kw)],
        out_shape=[jax.ShapeDtypeStruct((grp.batch, grp.rows, qw), BF16),
                   jax.ShapeDtypeStruct((grp.batch, kv, nb, HEAD_DIM, BLK), BF16),
                   jax.ShapeDtypeStruct((grp.batch, grp.rows, kw), BF16)],
        compiler_params=_cparams(("parallel", "parallel"), 32),
        name="attn_axial_prep",
    )(proj, proj, proj, proj, cos_t, sin_t, gq.reshape(1, HEAD_DIM).astype(F32), gk.reshape(1, HEAD_DIM).astype(F32))


def _flash_b_body(q_ref, k_ref, v_ref, o_ref, s_ref, m_ref, l_ref, acc_ref, *, m, n_real, kb, g):
    rows, tk = g * BLK, kb * BLK
    exp2_scale = HEAD_DIM ** -0.5 * math.log2(math.e)
    q = jnp.concatenate([q_ref[:, i * HEAD_DIM:(i + 1) * HEAD_DIM] for i in range(g)], axis=0)
    tiles = lambda x: [x[:, t * BLK:(t + 1) * BLK] for t in range(x.shape[1] // BLK)]

    meta_mask = lax.broadcasted_iota(jnp.int32, (rows, BLK), 1) >= PAD_ROWS
    s_meta = jnp.where(meta_mask, jnp.dot(q, k_ref[m], preferred_element_type=F32), NEG_INF)
    m_ref[...] = s_meta

    def phase1(c, carry):
        first = m + 1 + c * kb
        kc = jnp.concatenate([k_ref[first + t] for t in range(kb)], axis=1)
        s = jnp.dot(q, kc, preferred_element_type=F32)
        s_ref[c] = s
        m_ref[...] = jnp.maximum(m_ref[...], functools.reduce(jnp.maximum, tiles(s)))
        return carry

    lax.fori_loop(0, n_real // tk, phase1, 0, unroll=FLASH_UNROLL)
    mx = jnp.broadcast_to(jnp.max(m_ref[...], axis=-1, keepdims=True), (rows, BLK))
    m_ref[...] = mx
    p_meta = jnp.exp2((s_meta - mx) * exp2_scale)
    l_ref[...] = p_meta
    acc_ref[...] = jnp.dot(p_meta.astype(BF16), v_ref[m * BLK:(m + 1) * BLK, :], preferred_element_type=F32)

    def phase2(c, carry):
        mxc = m_ref[...]
        ps = [jnp.exp2((t - mxc) * exp2_scale) for t in tiles(s_ref[c])]
        l_ref[...] += functools.reduce(jnp.add, ps)
        p = jnp.concatenate([x.astype(BF16) for x in ps], axis=1)
        start = pl.multiple_of((m + 1 + c * kb) * BLK, BLK)
        acc_ref[...] += jnp.dot(p, v_ref[pl.ds(start, tk), :], preferred_element_type=F32)
        return carry

    lax.fori_loop(0, n_real // tk, phase2, 0, unroll=FLASH_UNROLL)
    out = acc_ref[...] / jnp.sum(l_ref[...], axis=-1, keepdims=True)
    o_ref[...] = jnp.concatenate([out[i * BLK:(i + 1) * BLK] for i in range(g)], axis=1).astype(o_ref.dtype)


def _flash_b(q, k, v, grp, heads, kv, total_rows, prev):
    g = heads // kv
    kb = FLASH_TK // BLK
    assert grp.n_real % FLASH_TK == 0
    rows = g * BLK
    base, nb = grp.base, grp.nb
    stat = pltpu.VMEM((rows, BLK), F32)
    in_specs = [pl.BlockSpec((None, BLK, g * HEAD_DIM), lambda b, kh, j: (b, j, kh)),
                pl.BlockSpec((None, None, nb, HEAD_DIM, BLK), lambda b, kh, j: (b, kh, 0, 0, 0)),
                pl.BlockSpec((None, grp.rows, HEAD_DIM), lambda b, kh, j: (b, 0, kh))]
    body = functools.partial(_flash_b_body, m=grp.m, n_real=grp.n_real, kb=kb, g=g)
    body, in_specs, args, aliases = _carry_rows(body, in_specs, [q, k, v], prev)
    return pl.pallas_call(
        body,
        grid=(grp.batch, kv, nb),
        in_specs=in_specs,
        out_specs=pl.BlockSpec((BLK, g * HEAD_DIM), lambda b, kh, j: (base + b * nb + j, kh)),
        out_shape=jax.ShapeDtypeStruct((total_rows, heads * HEAD_DIM), BF16),
        scratch_shapes=[pltpu.VMEM((grp.n_real // FLASH_TK, rows, FLASH_TK), F32), stat, stat, stat],
        input_output_aliases=aliases,
        compiler_params=_cparams(("parallel", "parallel", "arbitrary"), 48),
        name="attn_axial",
    )(*args)


def _ret_tables(l2d_ref, heads, tab_in, tab_q, tab_k, *, backward_only):
    ri = lax.broadcasted_iota(jnp.int32, (BLK, BLK), 0).astype(F32)
    ci = lax.broadcasted_iota(jnp.int32, (BLK, BLK), 1).astype(F32)
    for h in range(heads):
        lg_b = jnp.log1p(-jnp.exp2(l2d_ref[heads + h:heads + h + 1, :]))
        if backward_only:
            tab_q[h] = jnp.exp(lg_b * (BLK - ri))
            tab_k[h] = jnp.exp(lg_b * ri)
        else:
            lg_f = jnp.log1p(-jnp.exp2(l2d_ref[h:h + 1, :]))
            diff = ri - ci
            fwd = jnp.exp(lg_f * jnp.maximum(diff, 0.0))
            bwd = jnp.exp(lg_b * jnp.maximum(-diff, 0.0))
            tab_in[h] = jnp.where(diff >= 0, fwd, bwd)
            tab_q[h] = jnp.exp(lg_f * (ri + 1.0))
            tab_k[h] = jnp.exp(lg_f * (BLK - 1.0 - ri))


def _ret_bwd_body(l2d_ref, q_ref, k_ref, v_ref, y_ref, state, tab_q, tab_k, *, heads, nb, npad):
    b, j = pl.program_id(0), pl.program_id(1)
    jj = nb - 1 - j
    scale = HEAD_DIM ** -0.5

    @pl.when((b == 0) & (j == 0))
    def _():
        _ret_tables(l2d_ref, heads, None, tab_q, tab_k, backward_only=True)

    @pl.when(j == 0)
    def _():
        state[...] = jnp.zeros_like(state)

    valid = (jj * BLK + lax.broadcasted_iota(jnp.int32, (BLK, HEAD_DIM), 0)) >= npad
    for h in range(heads):
        hs = slice(h * HEAD_DIM, (h + 1) * HEAD_DIM)
        ks = jnp.where(valid, k_ref[:, hs].astype(F32) * scale, 0.0)
        vh = jnp.where(valid, v_ref[:, hs], jnp.zeros((), v_ref.dtype))
        st = state[h]
        y_ref[:, hs] = jnp.dot(q_ref[:, hs], st.astype(BF16), preferred_element_type=F32) * tab_q[h]
        kd = (ks * tab_k[h]).astype(BF16)
        upd = lax.dot_general(kd, vh, (((0,), (0,)), ((), ())), preferred_element_type=F32)
        state[h] = st * tab_q[h][0:1, :] + upd


def _ret_fwd_body(l2d_ref, q_ref, k_ref, v_ref, g_ref, yb_ref, gn_ref, o_ref, state, tab_in, tab_q, tab_k,
                  *, heads, npad):
    b, j = pl.program_id(0), pl.program_id(1)
    scale = HEAD_DIM ** -0.5

    @pl.when((b == 0) & (j == 0))
    def _():
        _ret_tables(l2d_ref, heads, tab_in, tab_q, tab_k, backward_only=False)

    @pl.when(j == 0)
    def _():
        state[...] = jnp.zeros_like(state)

    valid = (j * BLK + lax.broadcasted_iota(jnp.int32, (BLK, HEAD_DIM), 0)) >= npad
    for h in range(heads):
        hs = slice(h * HEAD_DIM, (h + 1) * HEAD_DIM)
        qh = q_ref[:, hs]
        ks = jnp.where(valid, k_ref[:, hs].astype(F32) * scale, 0.0)
        vh = jnp.where(valid, v_ref[:, hs], jnp.zeros((), v_ref.dtype))
        st = state[h]
        scores = lax.dot_general(qh, ks.astype(BF16), (((1,), (1,)), ((), ())), preferred_element_type=F32) * tab_in[h]
        inner = jnp.dot(scores.astype(BF16), vh, preferred_element_type=F32)
        cross = jnp.dot(qh, st.astype(BF16), preferred_element_type=F32) * tab_q[h]
        kd = (ks * tab_k[h]).astype(BF16)
        upd = lax.dot_general(kd, vh, (((0,), (0,)), ((), ())), preferred_element_type=F32)
        state[h] = st * tab_q[h][BLK - 1:BLK, :] + upd
        y = inner + cross + yb_ref[:, hs]
        mu = jnp.mean(y, axis=-1, keepdims=True)
        yc = y - mu
        var = jnp.mean(yc * yc, axis=-1, keepdims=True)
        yn = (yc * lax.rsqrt(var + GN_EPS)) * gn_ref[:, hs]
        o_ref[:, hs] = (jax.nn.silu(g_ref[:, hs].astype(F32)) * yn).astype(o_ref.dtype)


def _retention(proj, l2d, gn_w, grp, heads, col_q, col_k, col_v, col_g, prev):
    width = heads * HEAD_DIM
    assert all(c % heads == 0 for c in (col_q, col_k, col_v, col_g))
    base, nb = grp.base, grp.nb
    l2d_rows = jnp.broadcast_to(l2d.astype(F32).reshape(2 * heads, 1), (2 * heads, HEAD_DIM))
    l2d_spec = pl.BlockSpec((2 * heads, HEAD_DIM), lambda b, j: (0, 0))
    table = pltpu.VMEM((heads, BLK, BLK), F32)

    def in_spec(col, reverse):
        def imap(b, j):
            jj = nb - 1 - j if reverse else j
            return (base + b * nb + jj, col // heads)
        return pl.BlockSpec((BLK, width), imap)

    y_bwd = pl.pallas_call(
        functools.partial(_ret_bwd_body, heads=heads, nb=nb, npad=grp.npad),
        grid=(grp.batch, nb),
        in_specs=[l2d_spec, in_spec(col_q, True), in_spec(col_k, True), in_spec(col_v, True)],
        out_specs=pl.BlockSpec((BLK, width), lambda b, j: (b * nb + nb - 1 - j, 0)),
        out_shape=jax.ShapeDtypeStruct((grp.batch * grp.rows, width), F32),
        scratch_shapes=[table, table, table],
        compiler_params=_cparams(("arbitrary", "arbitrary"), 32),
        name="retention_bwd",
    )(l2d_rows, proj, proj, proj)
    in_specs = [l2d_spec, in_spec(col_q, False), in_spec(col_k, False), in_spec(col_v, False),
                in_spec(col_g, False), pl.BlockSpec((BLK, width), lambda b, j: (b * nb + j, 0)),
                pl.BlockSpec((1, width), lambda b, j: (0, 0))]
    body = functools.partial(_ret_fwd_body, heads=heads, npad=grp.npad)
    args = [l2d_rows, proj, proj, proj, proj, y_bwd, gn_w.reshape(1, width).astype(F32)]
    body, in_specs, args, aliases = _carry_rows(body, in_specs, args, prev)
    return pl.pallas_call(
        body,
        grid=(grp.batch, nb),
        in_specs=in_specs,
        out_specs=pl.BlockSpec((BLK, width), lambda b, j: (base + b * nb + j, 0)),
        out_shape=jax.ShapeDtypeStruct((proj.shape[0], width), BF16),
        scratch_shapes=[table, table, table, table],
        input_output_aliases=aliases,
        compiler_params=_cparams(("arbitrary", "arbitrary"), 32),
        name="retention_fwd",
    )(*args)


def _softplus(x):
    return jnp.maximum(x, 0.0) + jnp.log1p(jnp.exp(-jnp.abs(x)))


def _lru_block(x_ref, prev_ref, next_ref, cw_ref, cb_ref, gw_ref, gb_ref, lam_ref, carry_ref, jj, *,
               nblocks, npad, seq_rows, reverse):
    width = x_ref.shape[-1]
    tail = prev_ref.shape[0]
    ri = lax.broadcasted_iota(jnp.int32, (BLK, width), 0)
    pos = jj * BLK + ri
    valid = pos >= npad
    x = jnp.where(valid, x_ref[...].astype(F32), 0.0)
    pi = jj * BLK - tail + lax.broadcasted_iota(jnp.int32, (tail, width), 0)
    prev = jnp.where(pi >= npad, prev_ref[...].astype(F32), 0.0)
    ni = (jj + 1) * BLK + lax.broadcasted_iota(jnp.int32, (tail, width), 0)
    nxt = jnp.where(ni < seq_rows, next_ref[...].astype(F32), 0.0)
    p1 = prev[tail - 1:tail, :]
    p2 = prev[tail - 2:tail - 1, :]
    n0 = nxt[0:1, :]
    x_m1 = jnp.where(ri == 0, p1, pltpu.roll(x, 1, axis=0))
    x_m2 = jnp.where(ri == 0, p2, jnp.where(ri == 1, p1, pltpu.roll(x, 2, axis=0)))
    x_p1 = jnp.where(ri == BLK - 1, n0, pltpu.roll(x, BLK - 1, axis=0))
    xc = (cw_ref[0:1, :] * x_m2 + cw_ref[1:2, :] * x_m1 + cw_ref[2:3, :] * x + cw_ref[3:4, :] * x_p1) + cb_ref[...]
    xcb = xc.astype(BF16)
    rl = lax.broadcasted_iota(jnp.int32, (BLK, HEAD_DIM), 0)
    sub = lax.bitwise_and(rl, SUBLANES - 1)
    keeps = [(sh, (sub < SUBLANES - sh) if reverse else (sub >= sh)) for sh in (1, 2, 4)]
    ok = (jj * BLK + rl) >= npad
    outs = []
    for c in range(nblocks):
        cs = slice(c * HEAD_DIM, (c + 1) * HEAD_DIM)
        gr = jnp.dot(xcb[:, cs], gw_ref[0, c], preferred_element_type=F32) + gb_ref[0:1, cs]
        gi = jnp.dot(xcb[:, cs], gw_ref[1, c], preferred_element_type=F32) + gb_ref[1:2, cs]
        r = jax.nn.sigmoid(gr)
        i = jax.nn.sigmoid(gi)
        log_a = (-LRU_C * r) * _softplus(-lam_ref[:, cs])
        ea = jnp.exp(log_a)
        u = jnp.where(ok, jnp.sqrt(1.0 - ea * ea) * (i * xc[:, cs]), 0.0)
        a = jnp.where(ok, ea, 1.0)
        for sh, keep in keeps:
            shift = BLK - sh if reverse else sh
            a_s = pltpu.roll(a, shift, axis=0)
            u_s = pltpu.roll(u, shift, axis=0)
            u = jnp.where(keep, a * u_s + u, u)
            a = jnp.where(keep, a * a_s, a)
        carry = carry_ref[:, cs]
        groups = [None] * (BLK // SUBLANES)
        edge = 0 if reverse else SUBLANES - 1
        for v in (reversed(range(len(groups))) if reverse else range(len(groups))):
            rows = slice(v * SUBLANES, (v + 1) * SUBLANES)
            hv = u[rows] + a[rows] * carry
            carry = hv[edge:edge + 1, :]
            groups[v] = hv
        carry_ref[:, cs] = carry
        outs.append(jnp.concatenate(groups, axis=0))
    return outs


def _lru_bwd_body(x_ref, prev_ref, next_ref, cw_ref, cb_ref, gw_ref, gb_ref, lam_ref, h_ref, carry_ref,
                  *, nb, nblocks, npad):
    j = pl.program_id(1)

    @pl.when(j == 0)
    def _():
        carry_ref[...] = jnp.zeros_like(carry_ref)

    outs = _lru_block(x_ref, prev_ref, next_ref, cw_ref, cb_ref, gw_ref, gb_ref, lam_ref, carry_ref, nb - 1 - j,
                      nblocks=nblocks, npad=npad, seq_rows=nb * BLK, reverse=True)
    for c, hcol in enumerate(outs):
        h_ref[:, c * HEAD_DIM:(c + 1) * HEAD_DIM] = hcol


def _lru_fwd_body(x_ref, prev_ref, next_ref, y_ref, hb_ref, cw_ref, cb_ref, gw_ref, gb_ref, lam_ref, o_ref, carry_ref,
                  *, nb, nblocks, npad):
    j = pl.program_id(1)

    @pl.when(j == 0)
    def _():
        carry_ref[...] = jnp.zeros_like(carry_ref)

    outs = _lru_block(x_ref, prev_ref, next_ref, cw_ref, cb_ref, gw_ref, gb_ref, lam_ref, carry_ref, j,
                      nblocks=nblocks, npad=npad, seq_rows=nb * BLK, reverse=False)
    for c, hcol in enumerate(outs):
        cs = slice(c * HEAD_DIM, (c + 1) * HEAD_DIM)
        gate = jax.nn.gelu(y_ref[:, cs].astype(F32))
        o_ref[:, cs] = ((hcol + hb_ref[:, cs]) * gate).astype(o_ref.dtype)


def _hawk(proj, conv_w, conv_b, gate_w, gate_b, lam, grp, total_rows, col_x, col_y, prev):
    nblocks = gate_w.shape[2]
    width = nblocks * HEAD_DIM
    tail = 16
    per = BLK // tail
    assert col_x % nblocks == 0 and col_y % nblocks == 0
    base, nb = grp.base, grp.nb
    last_tail = total_rows // tail - 1

    def specs(reverse):
        def blk(b, j):
            return base + b * nb + (nb - 1 - j if reverse else j)
        return [pl.BlockSpec((BLK, width), lambda b, j: (blk(b, j), col_x // nblocks)),
                pl.BlockSpec((tail, width), lambda b, j: (jnp.maximum(blk(b, j) * per - 1, 0), col_x // nblocks)),
                pl.BlockSpec((tail, width), lambda b, j: (jnp.minimum((blk(b, j) + 1) * per, last_tail), col_x // nblocks))]

    def param_specs(d):
        return [pl.BlockSpec((conv_w.shape[0], width), lambda b, j: (0, 0)),
                pl.BlockSpec((1, width), lambda b, j: (0, 0)),
                pl.BlockSpec((None, 2, nblocks, HEAD_DIM, HEAD_DIM), lambda b, j: (d, 0, 0, 0, 0)),
                pl.BlockSpec((None, 2, width), lambda b, j: (d, 0, 0)),
                pl.BlockSpec((None, 1, width), lambda b, j: (d, 0, 0))]

    params = (conv_w.astype(F32), conv_b.reshape(1, width).astype(F32), gate_w, gate_b.astype(F32),
              lam.reshape(2, 1, width).astype(F32))
    carry = pltpu.VMEM((1, width), F32)
    h_bwd = pl.pallas_call(
        functools.partial(_lru_bwd_body, nb=nb, nblocks=nblocks, npad=grp.npad),
        grid=(grp.batch, nb),
        in_specs=specs(True) + param_specs(1),
        out_specs=pl.BlockSpec((BLK, width), lambda b, j: (b * nb + nb - 1 - j, 0)),
        out_shape=jax.ShapeDtypeStruct((grp.batch * grp.rows, width), F32),
        scratch_shapes=[carry],
        compiler_params=_cparams(("arbitrary", "arbitrary"), 32),
        name="rglru_bwd",
    )(proj, proj, proj, *params)
    in_specs = (specs(False)
                + [pl.BlockSpec((BLK, width), lambda b, j: (base + b * nb + j, col_y // nblocks)),
                   pl.BlockSpec((BLK, width), lambda b, j: (b * nb + j, 0))]
                + param_specs(0))
    body = functools.partial(_lru_fwd_body, nb=nb, nblocks=nblocks, npad=grp.npad)
    body, in_specs, args, aliases = _carry_rows(body, in_specs, [proj, proj, proj, proj, h_bwd, *params], prev)
    return pl.pallas_call(
        body,
        grid=(grp.batch, nb),
        in_specs=in_specs,
        out_specs=pl.BlockSpec((BLK, width), lambda b, j: (base + b * nb + j, 0)),
        out_shape=jax.ShapeDtypeStruct((total_rows, width), BF16),
        scratch_shapes=[carry],
        input_output_aliases=aliases,
        compiler_params=_cparams(("arbitrary", "arbitrary"), 32),
        name="rglru_fwd",
    )(*args)


def _plan(batches, seqs):
    per_tile = MM_TILE // BLK
    nbs = [1 + s // BLK for s in seqs]
    total = sum(b * nb for b, nb in zip(batches, nbs))
    extra = (-total) % per_tile
    m0 = extra // batches[0] if extra % batches[0] == 0 else 0
    groups, base = [], 0
    for gi, (b, s, nb) in enumerate(zip(batches, seqs, nbs)):
        m = m0 if gi == 0 else 0
        groups.append(Group(base=base, batch=b, nb=nb + m, m=m, n_real=s))
        base += b * (nb + m)
    return groups, base * BLK


def kernel(x_prompt, x_sample, meta_tokens, norm_mix_w, w_in, attn_sink, qk_norm_q, qk_norm_k, ret_log2_decay,
           ret_gn_w, lru_conv_w, lru_conv_b, lru_gate_w, lru_gate_b, lru_lambda, w_branch, w_out, norm_mlp_w,
           w_up, w_down, final_norm_w):
    xs = (x_prompt, x_sample)
    d = x_prompt.shape[-1]
    depth = w_in.shape[0]
    bw = w_branch.shape[2]
    heads = bw // HEAD_DIM
    kv = heads // 4
    assert all(x.shape[1] % BLK == 0 and x.shape[1] % GRID_W == 0 for x in xs) and bw % HEAD_DIM == 0
    groups, total_rows = _plan([x.shape[0] for x in xs], [x.shape[1] for x in xs])

    h = None
    for grp, x in zip(groups, xs):
        h = _embed(x, meta_tokens, grp, total_rows, h)
    mask_parts = [np.tile(np.arange(grp.rows) >= grp.npad, grp.batch) for grp in groups]
    row_mask = jnp.asarray(np.concatenate(mask_parts).astype(np.float32).reshape(total_rows, 1))

    hw, kw = heads * HEAD_DIM, kv * HEAD_DIM
    names = ("aq", "ak", "av", "bq", "bk", "bv", "cq", "ck", "cv", "cg", "dx", "dy", "gate")
    widths = (hw, kw, kw, hw, kw, kw, bw, bw, bw, bw, bw, bw, N_BRANCH * d)
    assert sum(widths) == w_in.shape[-1]
    col = {name: int(off) // HEAD_DIM for name, off in zip(names, np.cumsum((0,) + widths))}
    w_in_b, w_branch_b, w_out_b, w_up_b, w_down_b = (w.astype(BF16) for w in (w_in, w_branch, w_out, w_up, w_down))
    gate_w_b = lru_gate_w.astype(BF16)
    rope_tables = [_rope_tables(grp) for grp in groups]

    def layer_fn(l, h):
        layer = jnp.full((1,), l, jnp.int32)
        at = lambda p: lax.dynamic_index_in_dim(p, l, axis=0, keepdims=False)
        xn = _rmsnorm(h, at(norm_mix_w), BF16)
        proj = _matmul(layer, xn, w_in_b)
        oa = ob = oc = od = None
        for grp, tables in zip(groups, rope_tables):
            oa = _attn_a(proj, at(attn_sink), grp, heads, kv, col["aq"], col["ak"], col["av"], oa)
            qb, kb, vb = _prep_b(proj, at(qk_norm_q), at(qk_norm_k), tables, grp, heads, kv,
                                 col["bq"], col["bk"], col["bv"])
            ob = _flash_b(qb, kb, vb, grp, heads, kv, total_rows, ob)
            oc = _retention(proj, at(ret_log2_decay), at(ret_gn_w), grp, heads,
                            col["cq"], col["ck"], col["cv"], col["cg"], oc)
            od = _hawk(proj, at(lru_conv_w), at(lru_conv_b), at(gate_w_b), at(lru_gate_b), at(lru_lambda),
                       grp, total_rows, col["dx"], col["dy"], od)
        merged = _merge(layer, row_mask, (oa, ob, oc, od), w_branch_b, proj, col["gate"] * HEAD_DIM)
        h = _matmul(layer, merged, w_out_b, epilogue="residual", residual=h, out_dtype=F32)
        xn2 = _rmsnorm(h, at(norm_mlp_w), BF16)
        hid = _matmul(layer, xn2, w_up_b, epilogue="relu2")
        return _matmul(layer, hid, w_down_b, epilogue="residual", residual=h, out_dtype=F32)

    h = lax.fori_loop(0, depth, layer_fn, h)
    return tuple(_final_norm(h, final_norm_w, grp, x.dtype) for grp, x in zip(groups, xs))
```

```python
import functools
import math
from typing import NamedTuple

import numpy as np
import jax
import jax.numpy as jnp
from jax import lax
from jax.experimental import pallas as pl
from jax.experimental.pallas import tpu as pltpu

F32 = jnp.float32
BF16 = jnp.bfloat16

HEAD_DIM = 128
N_META = 16
BLK = 128
SUBLANES = 8
LANES = 128
PAD_ROWS = BLK - N_META
WINDOW = 128
GRID_W = 64
ROPE_THETA = 10000.0
ROPE_FREQS = HEAD_DIM // 4
CONV_LEFT = 2
LRU_C = 8.0
NORM_EPS = 1e-6
GN_EPS = 1e-5
NEG_INF = -1e30
N_BRANCH = 4

V7X_VMEM_BYTES = 64 * 1024 * 1024
MM_TILE = 1024
MM_TILE_K = 4096
FLASH_TK = 256
FLASH_UNROLL = 8


class Group(NamedTuple):
    base: int
    batch: int
    nb: int
    m: int
    n_real: int

    @property
    def npad(self):
        return self.m * BLK + PAD_ROWS

    @property
    def rows(self):
        return self.nb * BLK


def _cparams(sem, vmem_mib):
    assert vmem_mib * 1024 * 1024 < V7X_VMEM_BYTES
    return pltpu.CompilerParams(dimension_semantics=sem, vmem_limit_bytes=vmem_mib * 1024 * 1024)


def _largest_tile(n, cap):
    t = cap
    while n % t:
        t //= 2
    return t


def _carry_rows(body, in_specs, args, prev):
    if prev is None:
        return body, in_specs, args, {}
    n_in = len(in_specs)

    def body_without_prev(*refs):
        return body(*refs[:n_in], *refs[n_in + 1:])

    return body_without_prev, in_specs + [pl.BlockSpec(memory_space=pl.ANY)], args + [prev], {n_in: 0}


def _embed_body(x_ref, meta_ref, o_ref, *, m):
    j = pl.program_id(1)

    @pl.when(j > m)
    def _():
        o_ref[...] = x_ref[...].astype(o_ref.dtype)

    @pl.when(j == m)
    def _():
        o_ref[0:PAD_ROWS, :] = jnp.zeros((PAD_ROWS, o_ref.shape[1]), o_ref.dtype)
        o_ref[PAD_ROWS:BLK, :] = meta_ref[...].astype(o_ref.dtype)

    @pl.when(j < m)
    def _():
        o_ref[...] = jnp.zeros_like(o_ref)


def _embed(x, meta, grp, total_rows, prev):
    d = x.shape[-1]
    base, nb, m = grp.base, grp.nb, grp.m
    in_specs = [pl.BlockSpec((None, BLK, d), lambda b, j: (b, jnp.maximum(j - m - 1, 0), 0)),
                pl.BlockSpec((N_META, d), lambda b, j: (0, 0))]
    body, in_specs, args, aliases = _carry_rows(functools.partial(_embed_body, m=m), in_specs, [x, meta], prev)
    return pl.pallas_call(
        body,
        grid=(grp.batch, nb),
        in_specs=in_specs,
        out_specs=pl.BlockSpec((BLK, d), lambda b, j: (base + b * nb + j, 0)),
        out_shape=jax.ShapeDtypeStruct((total_rows, d), F32),
        input_output_aliases=aliases,
        compiler_params=_cparams(("parallel", "arbitrary"), 32),
        name="embed_tokens",
    )(*args)


def _rmsnorm_body(x_ref, w_ref, o_ref):
    x = x_ref[...].astype(F32)
    ms = jnp.mean(x * x, axis=-1, keepdims=True)
    o_ref[...] = ((x * lax.rsqrt(ms + NORM_EPS)) * w_ref[...]).astype(o_ref.dtype)


def _rmsnorm(h, w, out_dtype):
    t, d = h.shape
    tr = _largest_tile(t, 256)
    return pl.pallas_call(
        _rmsnorm_body,
        grid=(t // tr,),
        in_specs=[pl.BlockSpec((tr, d), lambda i: (i, 0)), pl.BlockSpec((1, d), lambda i: (0, 0))],
        out_specs=pl.BlockSpec((tr, d), lambda i: (i, 0)),
        out_shape=jax.ShapeDtypeStruct((t, d), out_dtype),
        compiler_params=_cparams(("parallel",), 32),
        name="rmsnorm",
    )(h, w.reshape(1, d).astype(F32))


def _final_norm(h, w, grp, out_dtype):
    _, d = h.shape
    nrb = grp.n_real // BLK
    return pl.pallas_call(
        _rmsnorm_body,
        grid=(grp.batch, nrb),
        in_specs=[pl.BlockSpec((BLK, d), lambda b, j: (grp.base + b * grp.nb + grp.m + 1 + j, 0)),
                  pl.BlockSpec((1, d), lambda b, j: (0, 0))],
        out_specs=pl.BlockSpec((None, BLK, d), lambda b, j: (b, j, 0)),
        out_shape=jax.ShapeDtypeStruct((grp.batch, grp.n_real, d), out_dtype),
        compiler_params=_cparams(("parallel", "parallel"), 32),
        name="final_norm",
    )(h, w.reshape(1, d).astype(F32))


def _mm_body(l_ref, a_ref, b_ref, *rest, epilogue, nk, row_norm, emit_normed):
    del l_ref
    rest = list(rest)
    ssq_ref = rest.pop(0) if row_norm else None
    r_ref = rest.pop(0) if epilogue == "residual" else None
    nw_ref = rest.pop(0) if emit_normed else None
    o_ref = rest.pop(0)
    hb_ref, sso_ref = (rest.pop(0), rest.pop(0)) if emit_normed else (None, None)
    scratch = rest

    def finish(acc):
        if row_norm:
            ms = jnp.sum(ssq_ref[...], axis=-1, keepdims=True) * (1.0 / (a_ref.shape[1] * nk))
            acc = acc * lax.rsqrt(ms + NORM_EPS)
        if epilogue == "relu2":
            acc = jnp.square(jnp.maximum(acc, 0.0))
        elif epilogue == "residual":
            acc = r_ref[...] + acc
        o_ref[...] = acc.astype(o_ref.dtype)
        if emit_normed:
            hb_ref[...] = (acc * nw_ref[...]).astype(hb_ref.dtype)
            sq = acc * acc
            sso_ref[...] = functools.reduce(
                jnp.add, [sq[:, t * LANES:(t + 1) * LANES] for t in range(sq.shape[1] // LANES)])

    prod = jnp.dot(a_ref[...], b_ref[...], preferred_element_type=F32)
    if nk == 1:
        finish(prod)
    elif epilogue == "residual" and o_ref.dtype == F32:
        k = pl.program_id(2)

        @pl.when(k == 0)
        def _():
            o_ref[...] = r_ref[...] + prod

        @pl.when(k > 0)
        def _():
            o_ref[...] += prod
    else:
        acc_ref, = scratch
        k = pl.program_id(2)

        @pl.when(k == 0)
        def _():
            acc_ref[...] = prod

        @pl.when(k > 0)
        def _():
            acc_ref[...] += prod

        @pl.when(k == nk - 1)
        def _():
            finish(acc_ref[...])


def _matmul(layer, a, w, *, epilogue="none", residual=None, out_dtype=BF16, row_ssq=None, next_norm_w=None):
    t, kdim = a.shape
    n = w.shape[-1]
    tm = _largest_tile(t, MM_TILE)
    tn = _largest_tile(n, MM_TILE)
    tk = _largest_tile(kdim, MM_TILE_K)
    nk = kdim // tk
    row_norm, emit_normed = row_ssq is not None, next_norm_w is not None
    out_accumulates = epilogue == "residual" and out_dtype == F32
    assert not emit_normed or (nk == 1 and epilogue == "residual")
    in_specs = [pl.BlockSpec((tm, tk), lambda i, j, k, l: (i, k)),
                pl.BlockSpec((None, tk, tn), lambda i, j, k, l: (l[0], k, j))]
    args = [a, w]
    aliases = {}
    if row_norm:
        in_specs.append(pl.BlockSpec((tm, row_ssq.shape[1]), lambda i, j, k, l: (i, 0)))
        args.append(row_ssq)
    if epilogue == "residual":
        in_specs.append(pl.BlockSpec((tm, tn), lambda i, j, k, l: (i, j)))
        args.append(residual)
        aliases = {len(args): 0}
    out_specs = [pl.BlockSpec((tm, tn), lambda i, j, k, l: (i, j))]
    out_shape = [jax.ShapeDtypeStruct((t, n), out_dtype)]
    if emit_normed:
        in_specs.append(pl.BlockSpec((1, tn), lambda i, j, k, l: (0, j)))
        args.append(next_norm_w.reshape(1, n).astype(F32))
        out_specs += [pl.BlockSpec((tm, tn), lambda i, j, k, l: (i, j)),
                      pl.BlockSpec((tm, LANES), lambda i, j, k, l: (i, j))]
        out_shape += [jax.ShapeDtypeStruct((t, n), BF16), jax.ShapeDtypeStruct((t, n // tn * LANES), F32)]
    outs = pl.pallas_call(
        functools.partial(_mm_body, epilogue=epilogue, nk=nk, row_norm=row_norm, emit_normed=emit_normed),
        grid_spec=pltpu.PrefetchScalarGridSpec(
            num_scalar_prefetch=1,
            grid=(t // tm, n // tn, nk),
            in_specs=in_specs,
            out_specs=out_specs,
            scratch_shapes=[pltpu.VMEM((tm, tn), F32)] if nk > 1 and not out_accumulates else [],
        ),
        out_shape=out_shape,
        input_output_aliases=aliases,
        compiler_params=_cparams(("parallel", "parallel", "arbitrary"), 62),
        name="matmul_" + epilogue,
    )(layer, *args)
    return outs if emit_normed else outs[0]


def _merge_body(l_ref, mask_ref, oa_ref, ob_ref, oc_ref, od_ref, wb_ref, ga_ref, gb_ref, gc_ref, gd_ref, out_ref):
    del l_ref
    acc = None
    for b, (o_ref, g_ref) in enumerate(((oa_ref, ga_ref), (ob_ref, gb_ref), (oc_ref, gc_ref), (od_ref, gd_ref))):
        proj = jnp.dot(o_ref[...], wb_ref[b], preferred_element_type=F32)
        term = proj * jax.nn.sigmoid(g_ref[...].astype(F32))
        acc = term if acc is None else acc + term
    out_ref[...] = jnp.where(mask_ref[...] > 0.0, acc, 0.0).astype(out_ref.dtype)


def _merge(layer, row_mask, outs, w_branch, proj, gate_col0):
    t, bw = outs[0].shape
    d = w_branch.shape[-1]
    tm = _largest_tile(t, MM_TILE)
    tn = _largest_tile(d, 512)
    assert gate_col0 % tn == 0 and d % tn == 0
    o_spec = pl.BlockSpec((tm, bw), lambda i, j, l: (i, 0))

    def gate_spec(b):
        return pl.BlockSpec((tm, tn), lambda i, j, l: (i, (gate_col0 + b * d) // tn + j))

    return pl.pallas_call(
        _merge_body,
        grid_spec=pltpu.PrefetchScalarGridSpec(
            num_scalar_prefetch=1,
            grid=(t // tm, d // tn),
            in_specs=[pl.BlockSpec((tm, 1), lambda i, j, l: (i, 0)), o_spec, o_spec, o_spec, o_spec,
                      pl.BlockSpec((None, N_BRANCH, bw, tn), lambda i, j, l: (l[0], 0, 0, j)),
                      gate_spec(0), gate_spec(1), gate_spec(2), gate_spec(3)],
            out_specs=pl.BlockSpec((tm, tn), lambda i, j, l: (i, j)),
        ),
        out_shape=jax.ShapeDtypeStruct((t, d), BF16),
        compiler_params=_cparams(("parallel", "arbitrary"), 56),
        name="branch_merge",
    )(layer, row_mask, *outs, w_branch, proj, proj, proj, proj)


def _attn_a_body(sink_ref, q_ref, km_ref, kp_ref, kc_ref, kn_ref, vm_ref, vp_ref, vc_ref, vn_ref, o_ref,
                 *, m, nb, heads, kv):
    j = pl.program_id(1)
    g = heads // kv
    scale = HEAD_DIM ** -0.5
    shape = (BLK, 4 * BLK)
    ri = lax.broadcasted_iota(jnp.int32, shape, 0)
    ci = lax.broadcasted_iota(jnp.int32, shape, 1)
    slot = lax.shift_right_logical(ci, 7)
    kin = lax.bitwise_and(ci, BLK - 1)
    dist = jnp.abs(ri - kin + (2 - slot) * BLK)
    kblk = j + slot - 2
    band_ok = (slot >= 1) & (kblk >= m + 1) & (kblk <= nb - 1) & (dist <= WINDOW)
    meta_ok = (slot == 0) & (kin >= PAD_ROWS)
    valid = band_ok | meta_ok
    distf = jnp.where(slot >= 1, dist, 0).astype(F32)
    for kh in range(kv):
        cs = slice(kh * HEAD_DIM, (kh + 1) * HEAD_DIM)
        kcat = jnp.concatenate([km_ref[:, cs], kp_ref[:, cs], kc_ref[:, cs], kn_ref[:, cs]], axis=0)
        vcat = jnp.concatenate([vm_ref[:, cs], vp_ref[:, cs], vc_ref[:, cs], vn_ref[:, cs]], axis=0)
        hss = [slice((kh * g + gi) * HEAD_DIM, (kh * g + gi + 1) * HEAD_DIM) for gi in range(g)]
        q4 = jnp.concatenate([q_ref[:, hs] for hs in hss], axis=0)
        s4 = lax.dot_general(q4, kcat, (((1,), (1,)), ((), ())), preferred_element_type=F32)
        ps = []
        for gi in range(g):
            h = kh * g + gi
            slope = 2.0 ** (-8.0 * (h + 1) / heads)
            s = jnp.where(valid, s4[gi * BLK:(gi + 1) * BLK] * scale - slope * distf, NEG_INF)
            sink = sink_ref[h]
            mx = jnp.maximum(jnp.max(s, axis=-1, keepdims=True), sink)
            p = jnp.exp(s - mx)
            den = jnp.sum(p, axis=-1, keepdims=True) + jnp.exp(sink - mx)
            ps.append((p / den).astype(BF16))
        o4 = jnp.dot(jnp.concatenate(ps, axis=0), vcat, preferred_element_type=F32)
        for gi, hs in enumerate(hss):
            o_ref[:, hs] = o4[gi * BLK:(gi + 1) * BLK].astype(o_ref.dtype)


def _attn_a(proj, sink, grp, heads, kv, col_q, col_k, col_v, prev):
    qw, kw = heads * HEAD_DIM, kv * HEAD_DIM
    assert col_q % heads == 0 and col_k % kv == 0 and col_v % kv == 0
    base, nb, m = grp.base, grp.nb, grp.m

    def kv_spec(col, which):
        def imap(b, j):
            if which == "meta":
                jj = m
            elif which == "prev":
                jj = jnp.maximum(j - 1, 0)
            elif which == "next":
                jj = jnp.minimum(j + 1, nb - 1)
            else:
                jj = j
            return (base + b * nb + jj, col // kv)
        return pl.BlockSpec((BLK, kw), imap)

    order = ("meta", "prev", "own", "next")
    in_specs = ([pl.BlockSpec(memory_space=pltpu.SMEM),
                 pl.BlockSpec((BLK, qw), lambda b, j: (base + b * nb + j, col_q // heads))]
                + [kv_spec(col_k, w) for w in order] + [kv_spec(col_v, w) for w in order])
    body = functools.partial(_attn_a_body, m=m, nb=nb, heads=heads, kv=kv)
    body, in_specs, args, aliases = _carry_rows(body, in_specs, [sink.astype(F32)] + [proj] * 9, prev)
    return pl.pallas_call(
        body,
        grid=(grp.batch, nb),
        in_specs=in_specs,
        out_specs=pl.BlockSpec((BLK, qw), lambda b, j: (base + b * nb + j, 0)),
        out_shape=jax.ShapeDtypeStruct((proj.shape[0], qw), BF16),
        input_output_aliases=aliases,
        compiler_params=_cparams(("parallel", "parallel"), 32),
        name="attn_window",
    )(*args)


def _rope_tables(grp):
    r = np.arange(grp.rows)
    t = r - grp.npad - N_META
    row = np.where(t >= 0, t // GRID_W, -1)
    col = np.where(t >= 0, t % GRID_W, np.clip(r - grp.npad, 0, N_META - 1))
    pos = jnp.asarray(np.stack([row, col], axis=-1), F32)
    inv_freq = ROPE_THETA ** (-jnp.arange(ROPE_FREQS, dtype=F32) / ROPE_FREQS)
    ang = pos[:, :, None] * inv_freq
    cos, sin = jnp.cos(ang), jnp.sin(ang)
    cos_t = jnp.concatenate([cos[:, 0], cos[:, 0], cos[:, 1], cos[:, 1]], axis=-1)
    sin_t = jnp.concatenate([-sin[:, 0], sin[:, 0], -sin[:, 1], sin[:, 1]], axis=-1)
    return cos_t, sin_t


def _prep_b_body(qa_ref, qb_ref, k_ref, v_ref, cos_ref, sin_ref, gq_ref, gk_ref, qo_ref, ko_ref, vo_ref, *, heads, kv):
    cos, sin = cos_ref[...], sin_ref[...]
    lane = lax.broadcasted_iota(jnp.int32, (BLK, HEAD_DIM), 1)
    first_half = lax.bitwise_and(lane, 2 * ROPE_FREQS - 1) < ROPE_FREQS

    def norm_rope(x, w):
        x = x.astype(F32)
        ms = jnp.mean(x * x, axis=-1, keepdims=True)
        xn = (x * lax.rsqrt(ms + NORM_EPS)) * w
        partner = jnp.where(first_half, pltpu.roll(xn, HEAD_DIM - ROPE_FREQS, axis=1), pltpu.roll(xn, ROPE_FREQS, axis=1))
        return xn * cos + partner * sin

    half = heads // 2
    for h in range(heads):
        src = qa_ref if h < half else qb_ref
        hl = h % half
        qo_ref[:, h * HEAD_DIM:(h + 1) * HEAD_DIM] = norm_rope(
            src[:, hl * HEAD_DIM:(hl + 1) * HEAD_DIM], gq_ref[...]).astype(qo_ref.dtype)
    for h in range(kv):
        hs = slice(h * HEAD_DIM, (h + 1) * HEAD_DIM)
        ko_ref[h] = norm_rope(k_ref[:, hs], gk_ref[...]).T.astype(ko_ref.dtype)
    for h in range(kv):
        vo_ref[:, 2 * h * HEAD_DIM:(2 * h + 1) * HEAD_DIM] = v_ref[:, h * HEAD_DIM:(h + 1) * HEAD_DIM]
        vo_ref[:, (2 * h + 1) * HEAD_DIM:(2 * h + 2) * HEAD_DIM] = jnp.ones((BLK, HEAD_DIM), vo_ref.dtype)


def _prep_b(proj, gq, gk, tables, grp, heads, kv, col_q, col_k, col_v):
    qw, kw = heads * HEAD_DIM, kv * HEAD_DIM
    half = heads // 2
    assert heads % 2 == 0 and col_q % half == 0 and col_k % kv == 0 and col_v % kv == 0
    base, nb = grp.base, grp.nb
    cos_t, sin_t = tables
    tab_spec = pl.BlockSpec((BLK, HEAD_DIM), lambda b, j: (j, 0))
    vec_spec = pl.BlockSpec((1, HEAD_DIM), lambda b, j: (0, 0))

    def in_spec(width, col, per):
        return pl.BlockSpec((BLK, width), lambda b, j: (base + b * nb + j, col // per))

    def out_spec(width):
        return pl.BlockSpec((None, BLK, width), lambda b, j: (b, j, 0))

    return pl.pallas_call(
        functools.partial(_prep_b_body, heads=heads, kv=kv),
        grid=(grp.batch, nb),
        in_specs=[in_spec(qw // 2, col_q, half), in_spec(qw // 2, col_q + half, half),
                  in_spec(kw, col_k, kv), in_spec(kw, col_v, kv), tab_spec, tab_spec, vec_spec, vec_spec],
        out_specs=[out_spec(qw),
                   pl.BlockSpec((None, kv, None, HEAD_DIM, BLK), lambda b, j: (b, 0, j, 0, 0)),
                   out_spec(2 * kw)],
        out_shape=[jax.ShapeDtypeStruct((grp.batch, grp.rows, qw), BF16),
                   jax.ShapeDtypeStruct((grp.batch, kv, nb, HEAD_DIM, BLK), BF16),
                   jax.ShapeDtypeStruct((grp.batch, grp.rows, 2 * kw), BF16)],
        compiler_params=_cparams(("parallel", "parallel"), 32),
        name="attn_axial_prep",
    )(proj, proj, proj, proj, cos_t, sin_t, gq.reshape(1, HEAD_DIM).astype(F32), gk.reshape(1, HEAD_DIM).astype(F32))


def _flash_b_body(q_ref, k_ref, v_ref, o_ref, s_ref, m_ref, acc_ref, *, m, n_real, kb, g):
    rows, tk = g * BLK, kb * BLK
    exp2_scale = HEAD_DIM ** -0.5 * math.log2(math.e)
    q = jnp.concatenate([q_ref[:, i * HEAD_DIM:(i + 1) * HEAD_DIM] for i in range(g)], axis=0)
    tiles = lambda x: [x[:, t * BLK:(t + 1) * BLK] for t in range(x.shape[1] // BLK)]

    meta_mask = lax.broadcasted_iota(jnp.int32, (rows, BLK), 1) >= PAD_ROWS
    s_meta = jnp.where(meta_mask, jnp.dot(q, k_ref[m], preferred_element_type=F32), NEG_INF)
    m_ref[...] = s_meta

    def phase1(c, carry):
        first = m + 1 + c * kb
        kc = jnp.concatenate([k_ref[first + t] for t in range(kb)], axis=1)
        s = jnp.dot(q, kc, preferred_element_type=F32)
        s_ref[c] = s
        m_ref[...] = jnp.maximum(m_ref[...], functools.reduce(jnp.maximum, tiles(s)))
        return carry

    lax.fori_loop(0, n_real // tk, phase1, 0, unroll=FLASH_UNROLL)
    mx = jnp.broadcast_to(jnp.max(m_ref[...], axis=-1, keepdims=True), (rows, BLK))
    m_ref[...] = mx
    p_meta = jnp.exp2((s_meta - mx) * exp2_scale)
    acc_ref[...] = jnp.dot(p_meta.astype(BF16), v_ref[m * BLK:(m + 1) * BLK, :], preferred_element_type=F32)

    def phase2(c, carry):
        mxc = m_ref[...]
        p = jnp.concatenate([jnp.exp2((t - mxc) * exp2_scale).astype(BF16) for t in tiles(s_ref[c])], axis=1)
        start = pl.multiple_of((m + 1 + c * kb) * BLK, BLK)
        acc_ref[...] += jnp.dot(p, v_ref[pl.ds(start, tk), :], preferred_element_type=F32)
        return carry

    lax.fori_loop(0, n_real // tk, phase2, 0, unroll=FLASH_UNROLL)
    acc = acc_ref[...]
    out = acc[:, :HEAD_DIM] / acc[:, HEAD_DIM:]
    o_ref[...] = jnp.concatenate([out[i * BLK:(i + 1) * BLK] for i in range(g)], axis=1).astype(o_ref.dtype)


def _flash_b(q, k, v, grp, heads, kv, total_rows, prev):
    g = heads // kv
    kb = FLASH_TK // BLK
    assert grp.n_real % FLASH_TK == 0
    rows = g * BLK
    base, nb = grp.base, grp.nb
    in_specs = [pl.BlockSpec((None, BLK, g * HEAD_DIM), lambda b, kh, j: (b, j, kh)),
                pl.BlockSpec((None, None, nb, HEAD_DIM, BLK), lambda b, kh, j: (b, kh, 0, 0, 0)),
                pl.BlockSpec((None, grp.rows, 2 * HEAD_DIM), lambda b, kh, j: (b, 0, kh))]
    body = functools.partial(_flash_b_body, m=grp.m, n_real=grp.n_real, kb=kb, g=g)
    body, in_specs, args, aliases = _carry_rows(body, in_specs, [q, k, v], prev)
    return pl.pallas_call(
        body,
        grid=(grp.batch, kv, nb),
        in_specs=in_specs,
        out_specs=pl.BlockSpec((BLK, g * HEAD_DIM), lambda b, kh, j: (base + b * nb + j, kh)),
        out_shape=jax.ShapeDtypeStruct((total_rows, heads * HEAD_DIM), BF16),
        scratch_shapes=[pltpu.VMEM((grp.n_real // FLASH_TK, rows, FLASH_TK), F32),
                        pltpu.VMEM((rows, BLK), F32), pltpu.VMEM((rows, 2 * HEAD_DIM), F32)],
        input_output_aliases=aliases,
        compiler_params=_cparams(("parallel", "parallel", "arbitrary"), 48),
        name="attn_axial",
    )(*args)


def _ret_tables(l2d_ref, heads, tab_in, tab_q, tab_k, *, backward_only):
    ri = lax.broadcasted_iota(jnp.int32, (BLK, BLK), 0).astype(F32)
    ci = lax.broadcasted_iota(jnp.int32, (BLK, BLK), 1).astype(F32)
    for h in range(heads):
        lg_b = jnp.log1p(-jnp.exp2(l2d_ref[heads + h:heads + h + 1, :]))
        if backward_only:
            tab_q[h] = jnp.exp(lg_b * (BLK - ri))
            tab_k[h] = jnp.exp(lg_b * ri)
        else:
            lg_f = jnp.log1p(-jnp.exp2(l2d_ref[h:h + 1, :]))
            diff = ri - ci
            fwd = jnp.exp(lg_f * jnp.maximum(diff, 0.0))
            bwd = jnp.exp(lg_b * jnp.maximum(-diff, 0.0))
            tab_in[h] = jnp.where(diff >= 0, fwd, bwd)
            tab_q[h] = jnp.exp(lg_f * (ri + 1.0))
            tab_k[h] = jnp.exp(lg_f * (BLK - 1.0 - ri))


def _ret_bwd_body(l2d_ref, q_ref, k_ref, v_ref, y_ref, state, tab_q, tab_k, *, heads, nb, npad):
    b, j = pl.program_id(0), pl.program_id(1)
    jj = nb - 1 - j
    scale = HEAD_DIM ** -0.5

    @pl.when((b == 0) & (j == 0))
    def _():
        _ret_tables(l2d_ref, heads, None, tab_q, tab_k, backward_only=True)

    @pl.when(j == 0)
    def _():
        state[...] = jnp.zeros_like(state)

    valid = (jj * BLK + lax.broadcasted_iota(jnp.int32, (BLK, HEAD_DIM), 0)) >= npad
    for h in range(heads):
        hs = slice(h * HEAD_DIM, (h + 1) * HEAD_DIM)
        ks = jnp.where(valid, k_ref[:, hs].astype(F32) * scale, 0.0)
        vh = jnp.where(valid, v_ref[:, hs], jnp.zeros((), v_ref.dtype))
        st = state[h]
        y_ref[:, hs] = jnp.dot(q_ref[:, hs], st.astype(BF16), preferred_element_type=F32) * tab_q[h]
        kd = (ks * tab_k[h]).astype(BF16)
        upd = lax.dot_general(kd, vh, (((0,), (0,)), ((), ())), preferred_element_type=F32)
        state[h] = st * tab_q[h][0:1, :] + upd


def _ret_fwd_body(l2d_ref, q_ref, k_ref, v_ref, g_ref, yb_ref, gn_ref, o_ref, state, tab_in, tab_q, tab_k,
                  *, heads, npad):
    b, j = pl.program_id(0), pl.program_id(1)
    scale = HEAD_DIM ** -0.5

    @pl.when((b == 0) & (j == 0))
    def _():
        _ret_tables(l2d_ref, heads, tab_in, tab_q, tab_k, backward_only=False)

    @pl.when(j == 0)
    def _():
        state[...] = jnp.zeros_like(state)

    valid = (j * BLK + lax.broadcasted_iota(jnp.int32, (BLK, HEAD_DIM), 0)) >= npad
    for h in range(heads):
        hs = slice(h * HEAD_DIM, (h + 1) * HEAD_DIM)
        qh = q_ref[:, hs]
        ks = jnp.where(valid, k_ref[:, hs].astype(F32) * scale, 0.0)
        vh = jnp.where(valid, v_ref[:, hs], jnp.zeros((), v_ref.dtype))
        st = state[h]
        scores = lax.dot_general(qh, ks.astype(BF16), (((1,), (1,)), ((), ())), preferred_element_type=F32) * tab_in[h]
        inner = jnp.dot(scores.astype(BF16), vh, preferred_element_type=F32)
        cross = jnp.dot(qh, st.astype(BF16), preferred_element_type=F32) * tab_q[h]
        kd = (ks * tab_k[h]).astype(BF16)
        upd = lax.dot_general(kd, vh, (((0,), (0,)), ((), ())), preferred_element_type=F32)
        state[h] = st * tab_q[h][BLK - 1:BLK, :] + upd
        y = inner + cross + yb_ref[:, hs]
        mu = jnp.mean(y, axis=-1, keepdims=True)
        yc = y - mu
        var = jnp.mean(yc * yc, axis=-1, keepdims=True)
        yn = (yc * lax.rsqrt(var + GN_EPS)) * gn_ref[:, hs]
        o_ref[:, hs] = (jax.nn.silu(g_ref[:, hs].astype(F32)) * yn).astype(o_ref.dtype)


def _retention(proj, l2d, gn_w, grp, heads, col_q, col_k, col_v, col_g, prev):
    width = heads * HEAD_DIM
    assert all(c % heads == 0 for c in (col_q, col_k, col_v, col_g))
    base, nb = grp.base, grp.nb
    l2d_rows = jnp.broadcast_to(l2d.astype(F32).reshape(2 * heads, 1), (2 * heads, HEAD_DIM))
    l2d_spec = pl.BlockSpec((2 * heads, HEAD_DIM), lambda b, j: (0, 0))
    table = pltpu.VMEM((heads, BLK, BLK), F32)

    def in_spec(col, reverse):
        def imap(b, j):
            jj = nb - 1 - j if reverse else j
            return (base + b * nb + jj, col // heads)
        return pl.BlockSpec((BLK, width), imap)

    y_bwd = pl.pallas_call(
        functools.partial(_ret_bwd_body, heads=heads, nb=nb, npad=grp.npad),
        grid=(grp.batch, nb),
        in_specs=[l2d_spec, in_spec(col_q, True), in_spec(col_k, True), in_spec(col_v, True)],
        out_specs=pl.BlockSpec((BLK, width), lambda b, j: (b * nb + nb - 1 - j, 0)),
        out_shape=jax.ShapeDtypeStruct((grp.batch * grp.rows, width), F32),
        scratch_shapes=[table, table, table],
        compiler_params=_cparams(("arbitrary", "arbitrary"), 32),
        name="retention_bwd",
    )(l2d_rows, proj, proj, proj)
    in_specs = [l2d_spec, in_spec(col_q, False), in_spec(col_k, False), in_spec(col_v, False),
                in_spec(col_g, False), pl.BlockSpec((BLK, width), lambda b, j: (b * nb + j, 0)),
                pl.BlockSpec((1, width), lambda b, j: (0, 0))]
    body = functools.partial(_ret_fwd_body, heads=heads, npad=grp.npad)
    args = [l2d_rows, proj, proj, proj, proj, y_bwd, gn_w.reshape(1, width).astype(F32)]
    body, in_specs, args, aliases = _carry_rows(body, in_specs, args, prev)
    return pl.pallas_call(
        body,
        grid=(grp.batch, nb),
        in_specs=in_specs,
        out_specs=pl.BlockSpec((BLK, width), lambda b, j: (base + b * nb + j, 0)),
        out_shape=jax.ShapeDtypeStruct((proj.shape[0], width), BF16),
        scratch_shapes=[table, table, table, table],
        input_output_aliases=aliases,
        compiler_params=_cparams(("arbitrary", "arbitrary"), 32),
        name="retention_fwd",
    )(*args)


def _softplus(x):
    return jnp.maximum(x, 0.0) + jnp.log1p(jnp.exp(-jnp.abs(x)))


def _roll_in_groups(x, shift):
    rows, lanes = x.shape
    grouped = x.reshape(rows // SUBLANES, SUBLANES, lanes)
    return pltpu.roll(grouped, shift, axis=1).reshape(rows, lanes)


def _lru_block(x_ref, prev_ref, next_ref, cw_ref, cb_ref, gw_ref, gb_ref, lam_ref, carry_ref, jj, *,
               nblocks, npad, seq_rows, reverse):
    width = x_ref.shape[-1]
    tail = prev_ref.shape[0]
    pi = jj * BLK - tail + lax.broadcasted_iota(jnp.int32, (tail, width), 0)
    prev = jnp.where(pi >= npad, prev_ref[...].astype(F32), 0.0)
    ni = (jj + 1) * BLK + lax.broadcasted_iota(jnp.int32, (tail, width), 0)
    nxt = jnp.where(ni < seq_rows, next_ref[...].astype(F32), 0.0)
    p1 = prev[tail - 1:tail, :]
    p2 = prev[tail - 2:tail - 1, :]
    n0 = nxt[0:1, :]
    rl = lax.broadcasted_iota(jnp.int32, (BLK, HEAD_DIM), 0)
    sub = lax.bitwise_and(rl, SUBLANES - 1)
    keeps = [(sh, (sub < SUBLANES - sh) if reverse else (sub >= sh)) for sh in (1, 2, 4)]
    ok = (jj * BLK + rl) >= npad
    row0, row1, row_last = rl == 0, rl == 1, rl == BLK - 1
    outs = []
    for c in range(nblocks):
        cs = slice(c * HEAD_DIM, (c + 1) * HEAD_DIM)
        x = jnp.where(ok, x_ref[:, cs].astype(F32), 0.0)
        x_m1 = jnp.where(row0, p1[:, cs], pltpu.roll(x, 1, axis=0))
        x_m2 = jnp.where(row0, p2[:, cs], jnp.where(row1, p1[:, cs], pltpu.roll(x, 2, axis=0)))
        x_p1 = jnp.where(row_last, n0[:, cs], pltpu.roll(x, BLK - 1, axis=0))
        xc = (cw_ref[0:1, cs] * x_m2 + cw_ref[1:2, cs] * x_m1 + cw_ref[2:3, cs] * x + cw_ref[3:4, cs] * x_p1) + cb_ref[:, cs]
        xcb = xc.astype(BF16)
        gr = jnp.dot(xcb, gw_ref[0, c], preferred_element_type=F32) + gb_ref[0:1, cs]
        gi = jnp.dot(xcb, gw_ref[1, c], preferred_element_type=F32) + gb_ref[1:2, cs]
        r = jax.nn.sigmoid(gr)
        i = jax.nn.sigmoid(gi)
        log_a = (-LRU_C * r) * _softplus(-lam_ref[:, cs])
        ea = jnp.exp(log_a)
        u = jnp.where(ok, jnp.sqrt(1.0 - ea * ea) * (i * xc), 0.0)
        a = jnp.where(ok, ea, 1.0)
        for sh, keep in keeps:
            shift = SUBLANES - sh if reverse else sh
            a_s = _roll_in_groups(a, shift)
            u_s = _roll_in_groups(u, shift)
            u = jnp.where(keep, a * u_s + u, u)
            a = jnp.where(keep, a * a_s, a)
        carry = carry_ref[:, cs]
        groups = [None] * (BLK // SUBLANES)
        edge = 0 if reverse else SUBLANES - 1
        for v in (reversed(range(len(groups))) if reverse else range(len(groups))):
            rows = slice(v * SUBLANES, (v + 1) * SUBLANES)
            hv = u[rows] + a[rows] * carry
            carry = hv[edge:edge + 1, :]
            groups[v] = hv
        carry_ref[:, cs] = carry
        outs.append(jnp.concatenate(groups, axis=0))
    return outs


def _lru_bwd_body(x_ref, prev_ref, next_ref, cw_ref, cb_ref, gw_ref, gb_ref, lam_ref, h_ref, carry_ref,
                  *, nb, nblocks, npad):
    j = pl.program_id(1)

    @pl.when(j == 0)
    def _():
        carry_ref[...] = jnp.zeros_like(carry_ref)

    outs = _lru_block(x_ref, prev_ref, next_ref, cw_ref, cb_ref, gw_ref, gb_ref, lam_ref, carry_ref, nb - 1 - j,
                      nblocks=nblocks, npad=npad, seq_rows=nb * BLK, reverse=True)
    for c, hcol in enumerate(outs):
        h_ref[:, c * HEAD_DIM:(c + 1) * HEAD_DIM] = hcol


def _lru_fwd_body(x_ref, prev_ref, next_ref, y_ref, hb_ref, cw_ref, cb_ref, gw_ref, gb_ref, lam_ref, o_ref, carry_ref,
                  *, nb, nblocks, npad):
    j = pl.program_id(1)

    @pl.when(j == 0)
    def _():
        carry_ref[...] = jnp.zeros_like(carry_ref)

    outs = _lru_block(x_ref, prev_ref, next_ref, cw_ref, cb_ref, gw_ref, gb_ref, lam_ref, carry_ref, j,
                      nblocks=nblocks, npad=npad, seq_rows=nb * BLK, reverse=False)
    for c, hcol in enumerate(outs):
        cs = slice(c * HEAD_DIM, (c + 1) * HEAD_DIM)
        gate = jax.nn.gelu(y_ref[:, cs].astype(F32))
        o_ref[:, cs] = ((hcol + hb_ref[:, cs]) * gate).astype(o_ref.dtype)


def _hawk(proj, conv_w, conv_b, gate_w, gate_b, lam, grp, total_rows, col_x, col_y, prev):
    nblocks = gate_w.shape[2]
    width = nblocks * HEAD_DIM
    tail = 16
    per = BLK // tail
    assert col_x % nblocks == 0 and col_y % nblocks == 0
    base, nb = grp.base, grp.nb
    last_tail = total_rows // tail - 1

    def specs(reverse):
        def blk(b, j):
            return base + b * nb + (nb - 1 - j if reverse else j)
        return [pl.BlockSpec((BLK, width), lambda b, j: (blk(b, j), col_x // nblocks)),
                pl.BlockSpec((tail, width), lambda b, j: (jnp.maximum(blk(b, j) * per - 1, 0), col_x // nblocks)),
                pl.BlockSpec((tail, width), lambda b, j: (jnp.minimum((blk(b, j) + 1) * per, last_tail), col_x // nblocks))]

    def param_specs(d):
        return [pl.BlockSpec((conv_w.shape[0], width), lambda b, j: (0, 0)),
                pl.BlockSpec((1, width), lambda b, j: (0, 0)),
                pl.BlockSpec((None, 2, nblocks, HEAD_DIM, HEAD_DIM), lambda b, j: (d, 0, 0, 0, 0)),
                pl.BlockSpec((None, 2, width), lambda b, j: (d, 0, 0)),
                pl.BlockSpec((None, 1, width), lambda b, j: (d, 0, 0))]

    params = (conv_w.astype(F32), conv_b.reshape(1, width).astype(F32), gate_w, gate_b.astype(F32),
              lam.reshape(2, 1, width).astype(F32))
    carry = pltpu.VMEM((1, width), F32)
    h_bwd = pl.pallas_call(
        functools.partial(_lru_bwd_body, nb=nb, nblocks=nblocks, npad=grp.npad),
        grid=(grp.batch, nb),
        in_specs=specs(True) + param_specs(1),
        out_specs=pl.BlockSpec((BLK, width), lambda b, j: (b * nb + nb - 1 - j, 0)),
        out_shape=jax.ShapeDtypeStruct((grp.batch * grp.rows, width), F32),
        scratch_shapes=[carry],
        compiler_params=_cparams(("arbitrary", "arbitrary"), 32),
        name="rglru_bwd",
    )(proj, proj, proj, *params)
    in_specs = (specs(False)
                + [pl.BlockSpec((BLK, width), lambda b, j: (base + b * nb + j, col_y // nblocks)),
                   pl.BlockSpec((BLK, width), lambda b, j: (b * nb + j, 0))]
                + param_specs(0))
    body = functools.partial(_lru_fwd_body, nb=nb, nblocks=nblocks, npad=grp.npad)
    body, in_specs, args, aliases = _carry_rows(body, in_specs, [proj, proj, proj, proj, h_bwd, *params], prev)
    return pl.pallas_call(
        body,
        grid=(grp.batch, nb),
        in_specs=in_specs,
        out_specs=pl.BlockSpec((BLK, width), lambda b, j: (base + b * nb + j, 0)),
        out_shape=jax.ShapeDtypeStruct((total_rows, width), BF16),
        scratch_shapes=[carry],
        input_output_aliases=aliases,
        compiler_params=_cparams(("arbitrary", "arbitrary"), 32),
        name="rglru_fwd",
    )(*args)


def _plan(batches, seqs):
    per_tile = MM_TILE // BLK
    nbs = [1 + s // BLK for s in seqs]
    total = sum(b * nb for b, nb in zip(batches, nbs))
    extra = (-total) % per_tile
    m0 = extra // batches[0] if extra % batches[0] == 0 else 0
    groups, base = [], 0
    for gi, (b, s, nb) in enumerate(zip(batches, seqs, nbs)):
        m = m0 if gi == 0 else 0
        groups.append(Group(base=base, batch=b, nb=nb + m, m=m, n_real=s))
        base += b * (nb + m)
    return groups, base * BLK


def kernel(x_prompt, x_sample, meta_tokens, norm_mix_w, w_in, attn_sink, qk_norm_q, qk_norm_k, ret_log2_decay,
           ret_gn_w, lru_conv_w, lru_conv_b, lru_gate_w, lru_gate_b, lru_lambda, w_branch, w_out, norm_mlp_w,
           w_up, w_down, final_norm_w):
    xs = (x_prompt, x_sample)
    d = x_prompt.shape[-1]
    depth = w_in.shape[0]
    bw = w_branch.shape[2]
    heads = bw // HEAD_DIM
    kv = heads // 4
    assert all(x.shape[1] % BLK == 0 and x.shape[1] % GRID_W == 0 for x in xs) and bw % HEAD_DIM == 0
    groups, total_rows = _plan([x.shape[0] for x in xs], [x.shape[1] for x in xs])

    h = None
    for grp, x in zip(groups, xs):
        h = _embed(x, meta_tokens, grp, total_rows, h)
    mask_parts = [np.tile(np.arange(grp.rows) >= grp.npad, grp.batch) for grp in groups]
    row_mask = jnp.asarray(np.concatenate(mask_parts).astype(np.float32).reshape(total_rows, 1))

    hw, kw = heads * HEAD_DIM, kv * HEAD_DIM
    names = ("aq", "ak", "av", "bq", "bk", "bv", "cq", "ck", "cv", "cg", "dx", "dy", "gate")
    widths = (hw, kw, kw, hw, kw, kw, bw, bw, bw, bw, bw, bw, N_BRANCH * d)
    assert sum(widths) == w_in.shape[-1]
    col = {name: int(off) // HEAD_DIM for name, off in zip(names, np.cumsum((0,) + widths))}
    w_in_b, w_branch_b, w_out_b, w_up_b, w_down_b = (w.astype(BF16) for w in (w_in, w_branch, w_out, w_up, w_down))
    gate_w_b = lru_gate_w.astype(BF16)
    rope_tables = [_rope_tables(grp) for grp in groups]

    def layer_fn(l, h):
        layer = jnp.full((1,), l, jnp.int32)
        at = lambda p: lax.dynamic_index_in_dim(p, l, axis=0, keepdims=False)
        xn = _rmsnorm(h, at(norm_mix_w), BF16)
        proj = _matmul(layer, xn, w_in_b)
        oa = ob = oc = od = None
        for grp, tables in zip(groups, rope_tables):
            oa = _attn_a(proj, at(attn_sink), grp, heads, kv, col["aq"], col["ak"], col["av"], oa)
            qb, kb, vb = _prep_b(proj, at(qk_norm_q), at(qk_norm_k), tables, grp, heads, kv,
                                 col["bq"], col["bk"], col["bv"])
            ob = _flash_b(qb, kb, vb, grp, heads, kv, total_rows, ob)
            oc = _retention(proj, at(ret_log2_decay), at(ret_gn_w), grp, heads,
                            col["cq"], col["ck"], col["cv"], col["cg"], oc)
            od = _hawk(proj, at(lru_conv_w), at(lru_conv_b), at(gate_w_b), at(lru_gate_b), at(lru_lambda),
                       grp, total_rows, col["dx"], col["dy"], od)
        merged = _merge(layer, row_mask, (oa, ob, oc, od), w_branch_b, proj, col["gate"] * HEAD_DIM)
        h, hw_b, ssq = _matmul(layer, merged, w_out_b, epilogue="residual", residual=h, out_dtype=F32,
                               next_norm_w=at(norm_mlp_w))
        hid = _matmul(layer, hw_b, w_up_b, epilogue="relu2", row_ssq=ssq)
        return _matmul(layer, hid, w_down_b, epilogue="residual", residual=h, out_dtype=F32)

    h = lax.fori_loop(0, depth, layer_fn, h)
    return tuple(_final_norm(h, final_norm_w, grp, x.dtype) for grp, x in zip(groups, xs))
```

```python
import functools
import math
from typing import NamedTuple

import numpy as np
import jax
import jax.numpy as jnp
from jax import lax
from jax.experimental import pallas as pl
from jax.experimental.pallas import tpu as pltpu

F32 = jnp.float32
BF16 = jnp.bfloat16

HEAD_DIM = 128
N_META = 16
BLK = 128
SUBLANES = 8
LANES = 128
PAD_ROWS = BLK - N_META
WINDOW = 128
GRID_W = 64
ROPE_THETA = 10000.0
ROPE_FREQS = HEAD_DIM // 4
CONV_LEFT = 2
LRU_C = 8.0
NORM_EPS = 1e-6
GN_EPS = 1e-5
NEG_INF = -1e30
N_BRANCH = 4

V7X_VMEM_BYTES = 64 * 1024 * 1024
MM_TILE = 1024
MM_TILE_K = 4096
FLASH_TK = 256
FLASH_UNROLL = 16


class Group(NamedTuple):
    base: int
    batch: int
    nb: int
    m: int
    n_real: int

    @property
    def npad(self):
        return self.m * BLK + PAD_ROWS

    @property
    def rows(self):
        return self.nb * BLK


def _cparams(sem, vmem_mib):
    assert vmem_mib * 1024 * 1024 < V7X_VMEM_BYTES
    return pltpu.CompilerParams(dimension_semantics=sem, vmem_limit_bytes=vmem_mib * 1024 * 1024)


def _largest_tile(n, cap):
    t = cap
    while n % t:
        t //= 2
    return t


def _carry_rows(body, in_specs, args, prev):
    if prev is None:
        return body, in_specs, args, {}
    n_in = len(in_specs)

    def body_without_prev(*refs):
        return body(*refs[:n_in], *refs[n_in + 1:])

    return body_without_prev, in_specs + [pl.BlockSpec(memory_space=pl.ANY)], args + [prev], {n_in: 0}


def _embed_body(x_ref, meta_ref, o_ref, *, m):
    j = pl.program_id(1)

    @pl.when(j > m)
    def _():
        o_ref[...] = x_ref[...].astype(o_ref.dtype)

    @pl.when(j == m)
    def _():
        o_ref[0:PAD_ROWS, :] = jnp.zeros((PAD_ROWS, o_ref.shape[1]), o_ref.dtype)
        o_ref[PAD_ROWS:BLK, :] = meta_ref[...].astype(o_ref.dtype)

    @pl.when(j < m)
    def _():
        o_ref[...] = jnp.zeros_like(o_ref)


def _embed(x, meta, grp, total_rows, prev):
    d = x.shape[-1]
    base, nb, m = grp.base, grp.nb, grp.m
    in_specs = [pl.BlockSpec((None, BLK, d), lambda b, j: (b, jnp.maximum(j - m - 1, 0), 0)),
                pl.BlockSpec((N_META, d), lambda b, j: (0, 0))]
    body, in_specs, args, aliases = _carry_rows(functools.partial(_embed_body, m=m), in_specs, [x, meta], prev)
    return pl.pallas_call(
        body,
        grid=(grp.batch, nb),
        in_specs=in_specs,
        out_specs=pl.BlockSpec((BLK, d), lambda b, j: (base + b * nb + j, 0)),
        out_shape=jax.ShapeDtypeStruct((total_rows, d), F32),
        input_output_aliases=aliases,
        compiler_params=_cparams(("parallel", "arbitrary"), 32),
        name="embed_tokens",
    )(*args)


def _rmsnorm_body(x_ref, w_ref, o_ref):
    x = x_ref[...].astype(F32)
    ms = jnp.mean(x * x, axis=-1, keepdims=True)
    o_ref[...] = ((x * lax.rsqrt(ms + NORM_EPS)) * w_ref[...]).astype(o_ref.dtype)


def _rmsnorm(h, w, out_dtype):
    t, d = h.shape
    tr = _largest_tile(t, 256)
    return pl.pallas_call(
        _rmsnorm_body,
        grid=(t // tr,),
        in_specs=[pl.BlockSpec((tr, d), lambda i: (i, 0)), pl.BlockSpec((1, d), lambda i: (0, 0))],
        out_specs=pl.BlockSpec((tr, d), lambda i: (i, 0)),
        out_shape=jax.ShapeDtypeStruct((t, d), out_dtype),
        compiler_params=_cparams(("parallel",), 32),
        name="rmsnorm",
    )(h, w.reshape(1, d).astype(F32))


def _final_norm(h, w, grp, out_dtype):
    _, d = h.shape
    nrb = grp.n_real // BLK
    return pl.pallas_call(
        _rmsnorm_body,
        grid=(grp.batch, nrb),
        in_specs=[pl.BlockSpec((BLK, d), lambda b, j: (grp.base + b * grp.nb + grp.m + 1 + j, 0)),
                  pl.BlockSpec((1, d), lambda b, j: (0, 0))],
        out_specs=pl.BlockSpec((None, BLK, d), lambda b, j: (b, j, 0)),
        out_shape=jax.ShapeDtypeStruct((grp.batch, grp.n_real, d), out_dtype),
        compiler_params=_cparams(("parallel", "parallel"), 32),
        name="final_norm",
    )(h, w.reshape(1, d).astype(F32))


def _mm_body(l_ref, a_ref, b_ref, *rest, epilogue, nk, row_norm, emit_normed):
    del l_ref
    rest = list(rest)
    ssq_ref = rest.pop(0) if row_norm else None
    r_ref = rest.pop(0) if epilogue == "residual" else None
    nw_ref = rest.pop(0) if emit_normed else None
    o_ref = rest.pop(0)
    hb_ref, sso_ref = (rest.pop(0), rest.pop(0)) if emit_normed else (None, None)
    scratch = rest

    def finish(acc):
        if row_norm:
            ms = jnp.sum(ssq_ref[...], axis=-1, keepdims=True) * (1.0 / (a_ref.shape[1] * nk))
            acc = acc * lax.rsqrt(ms + NORM_EPS)
        if epilogue == "relu2":
            acc = jnp.square(jnp.maximum(acc, 0.0))
        elif epilogue == "residual":
            acc = r_ref[...] + acc
        o_ref[...] = acc.astype(o_ref.dtype)
        if emit_normed:
            hb_ref[...] = (acc * nw_ref[...]).astype(hb_ref.dtype)
            sq = acc * acc
            sso_ref[...] = functools.reduce(
                jnp.add, [sq[:, t * LANES:(t + 1) * LANES] for t in range(sq.shape[1] // LANES)])

    def product():
        return jnp.dot(a_ref[...], b_ref[...], preferred_element_type=F32)

    if nk == 1:
        finish(product())
    elif epilogue == "residual" and o_ref.dtype == F32:
        k = pl.program_id(2)

        @pl.when(k == 0)
        def _():
            o_ref[...] = r_ref[...] + product()

        @pl.when(k > 0)
        def _():
            o_ref[...] += product()
    else:
        acc_ref, = scratch
        k = pl.program_id(2)
        prod = product()

        @pl.when(k == 0)
        def _():
            acc_ref[...] = prod

        @pl.when(k > 0)
        def _():
            acc_ref[...] += prod

        @pl.when(k == nk - 1)
        def _():
            finish(acc_ref[...])


def _matmul(layer, a, w, *, epilogue="none", residual=None, out_dtype=BF16, row_ssq=None, next_norm_w=None):
    t, kdim = a.shape
    n = w.shape[-1]
    tm = _largest_tile(t, MM_TILE)
    tn = _largest_tile(n, MM_TILE)
    tk = _largest_tile(kdim, MM_TILE_K)
    nk = kdim // tk
    row_norm, emit_normed = row_ssq is not None, next_norm_w is not None
    out_accumulates = epilogue == "residual" and out_dtype == F32
    assert not emit_normed or (nk == 1 and epilogue == "residual")
    in_specs = [pl.BlockSpec((tm, tk), lambda i, j, k, l: (i, k)),
                pl.BlockSpec((None, tk, tn), lambda i, j, k, l: (l[0], k, j))]
    args = [a, w]
    aliases = {}
    if row_norm:
        in_specs.append(pl.BlockSpec((tm, row_ssq.shape[1]), lambda i, j, k, l: (i, 0)))
        args.append(row_ssq)
    if epilogue == "residual":
        in_specs.append(pl.BlockSpec((tm, tn), lambda i, j, k, l: (i, j)))
        args.append(residual)
        aliases = {len(args): 0}
    out_specs = [pl.BlockSpec((tm, tn), lambda i, j, k, l: (i, j))]
    out_shape = [jax.ShapeDtypeStruct((t, n), out_dtype)]
    if emit_normed:
        in_specs.append(pl.BlockSpec((1, tn), lambda i, j, k, l: (0, j)))
        args.append(next_norm_w.reshape(1, n).astype(F32))
        out_specs += [pl.BlockSpec((tm, tn), lambda i, j, k, l: (i, j)),
                      pl.BlockSpec((tm, LANES), lambda i, j, k, l: (i, j))]
        out_shape += [jax.ShapeDtypeStruct((t, n), BF16), jax.ShapeDtypeStruct((t, n // tn * LANES), F32)]
    outs = pl.pallas_call(
        functools.partial(_mm_body, epilogue=epilogue, nk=nk, row_norm=row_norm, emit_normed=emit_normed),
        grid_spec=pltpu.PrefetchScalarGridSpec(
            num_scalar_prefetch=1,
            grid=(t // tm, n // tn, nk),
            in_specs=in_specs,
            out_specs=out_specs,
            scratch_shapes=[pltpu.VMEM((tm, tn), F32)] if nk > 1 and not out_accumulates else [],
        ),
        out_shape=out_shape,
        input_output_aliases=aliases,
        compiler_params=_cparams(("parallel", "parallel", "arbitrary"), 62),
        name="matmul_" + epilogue,
    )(layer, *args)
    return outs if emit_normed else outs[0]


def _merge_body(l_ref, mask_ref, oa_ref, ob_ref, oc_ref, od_ref, wb_ref, ga_ref, gb_ref, gc_ref, gd_ref, out_ref):
    del l_ref
    acc = None
    for b, (o_ref, g_ref) in enumerate(((oa_ref, ga_ref), (ob_ref, gb_ref), (oc_ref, gc_ref), (od_ref, gd_ref))):
        proj = jnp.dot(o_ref[...], wb_ref[b], preferred_element_type=F32)
        term = proj * jax.nn.sigmoid(g_ref[...].astype(F32))
        acc = term if acc is None else acc + term
    out_ref[...] = jnp.where(mask_ref[...] > 0.0, acc, 0.0).astype(out_ref.dtype)


def _merge(layer, row_mask, outs, w_branch, proj, gate_col0):
    t, bw = outs[0].shape
    d = w_branch.shape[-1]
    tm = _largest_tile(t, MM_TILE)
    tn = _largest_tile(d, 512)
    assert gate_col0 % tn == 0 and d % tn == 0
    o_spec = pl.BlockSpec((tm, bw), lambda i, j, l: (i, 0))

    def gate_spec(b):
        return pl.BlockSpec((tm, tn), lambda i, j, l: (i, (gate_col0 + b * d) // tn + j))

    return pl.pallas_call(
        _merge_body,
        grid_spec=pltpu.PrefetchScalarGridSpec(
            num_scalar_prefetch=1,
            grid=(t // tm, d // tn),
            in_specs=[pl.BlockSpec((tm, 1), lambda i, j, l: (i, 0)), o_spec, o_spec, o_spec, o_spec,
                      pl.BlockSpec((None, N_BRANCH, bw, tn), lambda i, j, l: (l[0], 0, 0, j)),
                      gate_spec(0), gate_spec(1), gate_spec(2), gate_spec(3)],
            out_specs=pl.BlockSpec((tm, tn), lambda i, j, l: (i, j)),
        ),
        out_shape=jax.ShapeDtypeStruct((t, d), BF16),
        compiler_params=_cparams(("parallel", "arbitrary"), 56),
        name="branch_merge",
    )(layer, row_mask, *outs, w_branch, proj, proj, proj, proj)


def _attn_a_body(sink_ref, q_ref, km_ref, kp_ref, kc_ref, kn_ref, vm_ref, vp_ref, vc_ref, vn_ref, o_ref,
                 *, m, nb, heads, kv):
    j = pl.program_id(1)
    g = heads // kv
    scale = HEAD_DIM ** -0.5
    shape = (BLK, 4 * BLK)
    ri = lax.broadcasted_iota(jnp.int32, shape, 0)
    ci = lax.broadcasted_iota(jnp.int32, shape, 1)
    slot = lax.shift_right_logical(ci, 7)
    kin = lax.bitwise_and(ci, BLK - 1)
    dist = jnp.abs(ri - kin + (2 - slot) * BLK)
    kblk = j + slot - 2
    band_ok = (slot >= 1) & (kblk >= m + 1) & (kblk <= nb - 1) & (dist <= WINDOW)
    meta_ok = (slot == 0) & (kin >= PAD_ROWS)
    valid = band_ok | meta_ok
    distf = jnp.where(slot >= 1, dist, 0).astype(F32)
    for kh in range(kv):
        cs = slice(kh * HEAD_DIM, (kh + 1) * HEAD_DIM)
        kcat = jnp.concatenate([km_ref[:, cs], kp_ref[:, cs], kc_ref[:, cs], kn_ref[:, cs]], axis=0)
        vcat = jnp.concatenate([vm_ref[:, cs], vp_ref[:, cs], vc_ref[:, cs], vn_ref[:, cs]], axis=0)
        hss = [slice((kh * g + gi) * HEAD_DIM, (kh * g + gi + 1) * HEAD_DIM) for gi in range(g)]
        q4 = jnp.concatenate([q_ref[:, hs] for hs in hss], axis=0)
        s4 = lax.dot_general(q4, kcat, (((1,), (1,)), ((), ())), preferred_element_type=F32)
        ps = []
        for gi in range(g):
            h = kh * g + gi
            slope = 2.0 ** (-8.0 * (h + 1) / heads)
            s = jnp.where(valid, s4[gi * BLK:(gi + 1) * BLK] * scale - slope * distf, NEG_INF)
            sink = sink_ref[h]
            mx = jnp.maximum(jnp.max(s, axis=-1, keepdims=True), sink)
            p = jnp.exp(s - mx)
            den = jnp.sum(p, axis=-1, keepdims=True) + jnp.exp(sink - mx)
            ps.append((p / den).astype(BF16))
        o4 = jnp.dot(jnp.concatenate(ps, axis=0), vcat, preferred_element_type=F32)
        for gi, hs in enumerate(hss):
            o_ref[:, hs] = o4[gi * BLK:(gi + 1) * BLK].astype(o_ref.dtype)


def _attn_a(proj, sink, grp, heads, kv, col_q, col_k, col_v, prev):
    qw, kw = heads * HEAD_DIM, kv * HEAD_DIM
    assert col_q % heads == 0 and col_k % kv == 0 and col_v % kv == 0
    base, nb, m = grp.base, grp.nb, grp.m

    def kv_spec(col, which):
        def imap(b, j):
            if which == "meta":
                jj = m
            elif which == "prev":
                jj = jnp.maximum(j - 1, 0)
            elif which == "next":
                jj = jnp.minimum(j + 1, nb - 1)
            else:
                jj = j
            return (base + b * nb + jj, col // kv)
        return pl.BlockSpec((BLK, kw), imap)

    order = ("meta", "prev", "own", "next")
    in_specs = ([pl.BlockSpec(memory_space=pltpu.SMEM),
                 pl.BlockSpec((BLK, qw), lambda b, j: (base + b * nb + j, col_q // heads))]
                + [kv_spec(col_k, w) for w in order] + [kv_spec(col_v, w) for w in order])
    body = functools.partial(_attn_a_body, m=m, nb=nb, heads=heads, kv=kv)
    body, in_specs, args, aliases = _carry_rows(body, in_specs, [sink.astype(F32)] + [proj] * 9, prev)
    return pl.pallas_call(
        body,
        grid=(grp.batch, nb),
        in_specs=in_specs,
        out_specs=pl.BlockSpec((BLK, qw), lambda b, j: (base + b * nb + j, 0)),
        out_shape=jax.ShapeDtypeStruct((proj.shape[0], qw), BF16),
        input_output_aliases=aliases,
        compiler_params=_cparams(("parallel", "parallel"), 32),
        name="attn_window",
    )(*args)


def _rope_tables(grp):
    r = np.arange(grp.rows)
    t = r - grp.npad - N_META
    row = np.where(t >= 0, t // GRID_W, -1)
    col = np.where(t >= 0, t % GRID_W, np.clip(r - grp.npad, 0, N_META - 1))
    pos = jnp.asarray(np.stack([row, col], axis=-1), F32)
    inv_freq = ROPE_THETA ** (-jnp.arange(ROPE_FREQS, dtype=F32) / ROPE_FREQS)
    ang = pos[:, :, None] * inv_freq
    cos, sin = jnp.cos(ang), jnp.sin(ang)
    cos_t = jnp.concatenate([cos[:, 0], cos[:, 0], cos[:, 1], cos[:, 1]], axis=-1)
    sin_t = jnp.concatenate([-sin[:, 0], sin[:, 0], -sin[:, 1], sin[:, 1]], axis=-1)
    return cos_t, sin_t


def _prep_b_body(qa_ref, qb_ref, k_ref, v_ref, cos_ref, sin_ref, gq_ref, gk_ref, qo_ref, ko_ref, vo_ref, *, heads, kv):
    cos, sin = cos_ref[...], sin_ref[...]
    lane = lax.broadcasted_iota(jnp.int32, (BLK, HEAD_DIM), 1)
    first_half = lax.bitwise_and(lane, 2 * ROPE_FREQS - 1) < ROPE_FREQS

    def norm_rope(x, w):
        x = x.astype(F32)
        ms = jnp.mean(x * x, axis=-1, keepdims=True)
        xn = (x * lax.rsqrt(ms + NORM_EPS)) * w
        partner = jnp.where(first_half, pltpu.roll(xn, HEAD_DIM - ROPE_FREQS, axis=1), pltpu.roll(xn, ROPE_FREQS, axis=1))
        return xn * cos + partner * sin

    half = heads // 2
    for h in range(heads):
        src = qa_ref if h < half else qb_ref
        hl = h % half
        qo_ref[:, h * HEAD_DIM:(h + 1) * HEAD_DIM] = norm_rope(
            src[:, hl * HEAD_DIM:(hl + 1) * HEAD_DIM], gq_ref[...]).astype(qo_ref.dtype)
    for h in range(kv):
        hs = slice(h * HEAD_DIM, (h + 1) * HEAD_DIM)
        ko_ref[h] = norm_rope(k_ref[:, hs], gk_ref[...]).T.astype(ko_ref.dtype)
    for h in range(kv):
        vo_ref[:, 2 * h * HEAD_DIM:(2 * h + 1) * HEAD_DIM] = v_ref[:, h * HEAD_DIM:(h + 1) * HEAD_DIM]
        vo_ref[:, (2 * h + 1) * HEAD_DIM:(2 * h + 2) * HEAD_DIM] = jnp.ones((BLK, HEAD_DIM), vo_ref.dtype)


def _prep_b(proj, gq, gk, tables, grp, heads, kv, col_q, col_k, col_v):
    qw, kw = heads * HEAD_DIM, kv * HEAD_DIM
    half = heads // 2
    assert heads % 2 == 0 and col_q % half == 0 and col_k % kv == 0 and col_v % kv == 0
    base, nb = grp.base, grp.nb
    cos_t, sin_t = tables
    tab_spec = pl.BlockSpec((BLK, HEAD_DIM), lambda b, j: (j, 0))
    vec_spec = pl.BlockSpec((1, HEAD_DIM), lambda b, j: (0, 0))

    def in_spec(width, col, per):
        return pl.BlockSpec((BLK, width), lambda b, j: (base + b * nb + j, col // per))

    def out_spec(width):
        return pl.BlockSpec((None, BLK, width), lambda b, j: (b, j, 0))

    return pl.pallas_call(
        functools.partial(_prep_b_body, heads=heads, kv=kv),
        grid=(grp.batch, nb),
        in_specs=[in_spec(qw // 2, col_q, half), in_spec(qw // 2, col_q + half, half),
                  in_spec(kw, col_k, kv), in_spec(kw, col_v, kv), tab_spec, tab_spec, vec_spec, vec_spec],
        out_specs=[out_spec(qw),
                   pl.BlockSpec((None, kv, None, HEAD_DIM, BLK), lambda b, j: (b, 0, j, 0, 0)),
                   out_spec(2 * kw)],
        out_shape=[jax.ShapeDtypeStruct((grp.batch, grp.rows, qw), BF16),
                   jax.ShapeDtypeStruct((grp.batch, kv, nb, HEAD_DIM, BLK), BF16),
                   jax.ShapeDtypeStruct((grp.batch, grp.rows, 2 * kw), BF16)],
        compiler_params=_cparams(("parallel", "parallel"), 32),
        name="attn_axial_prep",
    )(proj, proj, proj, proj, cos_t, sin_t, gq.reshape(1, HEAD_DIM).astype(F32), gk.reshape(1, HEAD_DIM).astype(F32))


def _flash_b_body(q_ref, k_ref, v_ref, o_ref, s_ref, m_ref, acc_ref, *, m, n_real, kb, g):
    rows, tk = g * BLK, kb * BLK
    exp2_scale = HEAD_DIM ** -0.5 * math.log2(math.e)
    q = jnp.concatenate([q_ref[:, i * HEAD_DIM:(i + 1) * HEAD_DIM] for i in range(g)], axis=0)
    tiles = lambda x: [x[:, t * BLK:(t + 1) * BLK] for t in range(x.shape[1] // BLK)]

    meta_mask = lax.broadcasted_iota(jnp.int32, (rows, BLK), 1) >= PAD_ROWS
    s_meta = jnp.where(meta_mask, jnp.dot(q, k_ref[m], preferred_element_type=F32), NEG_INF)
    m_ref[...] = s_meta

    def phase1(c, carry):
        first = m + 1 + c * kb
        kc = jnp.concatenate([k_ref[first + t] for t in range(kb)], axis=1)
        s = jnp.dot(q, kc, preferred_element_type=F32)
        s_ref[c] = s
        m_ref[...] = jnp.maximum(m_ref[...], functools.reduce(jnp.maximum, tiles(s)))
        return carry

    unroll = max(1, min(FLASH_UNROLL, n_real // tk // 2))
    lax.fori_loop(0, n_real // tk, phase1, 0, unroll=unroll)
    mx = jnp.broadcast_to(jnp.max(m_ref[...], axis=-1, keepdims=True), (rows, BLK))
    m_ref[...] = mx
    p_meta = jnp.exp2((s_meta - mx) * exp2_scale)
    acc_ref[...] = jnp.dot(p_meta.astype(BF16), v_ref[m * BLK:(m + 1) * BLK, :], preferred_element_type=F32)

    def phase2(c, carry):
        mxc = m_ref[...]
        p = jnp.concatenate([jnp.exp2((t - mxc) * exp2_scale).astype(BF16) for t in tiles(s_ref[c])], axis=1)
        start = pl.multiple_of((m + 1 + c * kb) * BLK, BLK)
        acc_ref[...] += jnp.dot(p, v_ref[pl.ds(start, tk), :], preferred_element_type=F32)
        return carry

    lax.fori_loop(0, n_real // tk, phase2, 0, unroll=unroll)
    acc = acc_ref[...]
    out = acc[:, :HEAD_DIM] / acc[:, HEAD_DIM:]
    o_ref[...] = jnp.concatenate([out[i * BLK:(i + 1) * BLK] for i in range(g)], axis=1).astype(o_ref.dtype)


def _flash_b(q, k, v, grp, heads, kv, total_rows, prev):
    g = heads // kv
    kb = FLASH_TK // BLK
    assert grp.n_real % FLASH_TK == 0
    rows = g * BLK
    base, nb = grp.base, grp.nb
    in_specs = [pl.BlockSpec((None, BLK, g * HEAD_DIM), lambda b, kh, j: (b, j, kh)),
                pl.BlockSpec((None, None, nb, HEAD_DIM, BLK), lambda b, kh, j: (b, kh, 0, 0, 0)),
                pl.BlockSpec((None, grp.rows, 2 * HEAD_DIM), lambda b, kh, j: (b, 0, kh))]
    body = functools.partial(_flash_b_body, m=grp.m, n_real=grp.n_real, kb=kb, g=g)
    body, in_specs, args, aliases = _carry_rows(body, in_specs, [q, k, v], prev)
    return pl.pallas_call(
        body,
        grid=(grp.batch, kv, nb),
        in_specs=in_specs,
        out_specs=pl.BlockSpec((BLK, g * HEAD_DIM), lambda b, kh, j: (base + b * nb + j, kh)),
        out_shape=jax.ShapeDtypeStruct((total_rows, heads * HEAD_DIM), BF16),
        scratch_shapes=[pltpu.VMEM((grp.n_real // FLASH_TK, rows, FLASH_TK), F32),
                        pltpu.VMEM((rows, BLK), F32), pltpu.VMEM((rows, 2 * HEAD_DIM), F32)],
        input_output_aliases=aliases,
        compiler_params=_cparams(("parallel", "parallel", "arbitrary"), 48),
        name="attn_axial",
    )(*args)


def _ret_tables(l2d_ref, heads, tab_in, tab_q, tab_k, *, backward_only):
    ri = lax.broadcasted_iota(jnp.int32, (BLK, BLK), 0).astype(F32)
    ci = lax.broadcasted_iota(jnp.int32, (BLK, BLK), 1).astype(F32)
    for h in range(heads):
        lg_b = jnp.log1p(-jnp.exp2(l2d_ref[heads + h:heads + h + 1, :]))
        if backward_only:
            tab_q[h] = jnp.exp(lg_b * (BLK - ri))
            tab_k[h] = jnp.exp(lg_b * ri)
        else:
            lg_f = jnp.log1p(-jnp.exp2(l2d_ref[h:h + 1, :]))
            diff = ri - ci
            fwd = jnp.exp(lg_f * jnp.maximum(diff, 0.0))
            bwd = jnp.exp(lg_b * jnp.maximum(-diff, 0.0))
            tab_in[h] = jnp.where(diff >= 0, fwd, bwd)
            tab_q[h] = jnp.exp(lg_f * (ri + 1.0))
            tab_k[h] = jnp.exp(lg_f * (BLK - 1.0 - ri))


def _ret_bwd_body(l2d_ref, q_ref, k_ref, v_ref, y_ref, state, tab_q, tab_k, *, heads, nb, npad):
    b, j = pl.program_id(0), pl.program_id(1)
    jj = nb - 1 - j
    scale = HEAD_DIM ** -0.5

    @pl.when((b == 0) & (j == 0))
    def _():
        _ret_tables(l2d_ref, heads, None, tab_q, tab_k, backward_only=True)

    @pl.when(j == 0)
    def _():
        state[...] = jnp.zeros_like(state)

    valid = (jj * BLK + lax.broadcasted_iota(jnp.int32, (BLK, HEAD_DIM), 0)) >= npad
    for h in range(heads):
        hs = slice(h * HEAD_DIM, (h + 1) * HEAD_DIM)
        ks = jnp.where(valid, k_ref[:, hs].astype(F32) * scale, 0.0)
        vh = jnp.where(valid, v_ref[:, hs], jnp.zeros((), v_ref.dtype))
        st = state[h]
        y_ref[:, hs] = jnp.dot(q_ref[:, hs], st.astype(BF16), preferred_element_type=F32) * tab_q[h]
        kd = (ks * tab_k[h]).astype(BF16)
        upd = lax.dot_general(kd, vh, (((0,), (0,)), ((), ())), preferred_element_type=F32)
        state[h] = st * tab_q[h][0:1, :] + upd


def _ret_fwd_body(l2d_ref, q_ref, k_ref, v_ref, g_ref, yb_ref, gn_ref, o_ref, state, tab_in, tab_q, tab_k,
                  *, heads, npad):
    b, j = pl.program_id(0), pl.program_id(1)
    scale = HEAD_DIM ** -0.5

    @pl.when((b == 0) & (j == 0))
    def _():
        _ret_tables(l2d_ref, heads, tab_in, tab_q, tab_k, backward_only=False)

    @pl.when(j == 0)
    def _():
        state[...] = jnp.zeros_like(state)

    valid = (j * BLK + lax.broadcasted_iota(jnp.int32, (BLK, HEAD_DIM), 0)) >= npad
    for h in range(heads):
        hs = slice(h * HEAD_DIM, (h + 1) * HEAD_DIM)
        qh = q_ref[:, hs]
        ks = jnp.where(valid, k_ref[:, hs].astype(F32) * scale, 0.0)
        vh = jnp.where(valid, v_ref[:, hs], jnp.zeros((), v_ref.dtype))
        st = state[h]
        scores = lax.dot_general(qh, ks.astype(BF16), (((1,), (1,)), ((), ())), preferred_element_type=F32) * tab_in[h]
        inner = jnp.dot(scores.astype(BF16), vh, preferred_element_type=F32)
        cross = jnp.dot(qh, st.astype(BF16), preferred_element_type=F32) * tab_q[h]
        kd = (ks * tab_k[h]).astype(BF16)
        upd = lax.dot_general(kd, vh, (((0,), (0,)), ((), ())), preferred_element_type=F32)
        state[h] = st * tab_q[h][BLK - 1:BLK, :] + upd
        y = inner + cross + yb_ref[:, hs]
        mu = jnp.mean(y, axis=-1, keepdims=True)
        yc = y - mu
        var = jnp.mean(yc * yc, axis=-1, keepdims=True)
        yn = (yc * lax.rsqrt(var + GN_EPS)) * gn_ref[:, hs]
        o_ref[:, hs] = (jax.nn.silu(g_ref[:, hs].astype(F32)) * yn).astype(o_ref.dtype)


def _retention(proj, l2d, gn_w, grp, heads, col_q, col_k, col_v, col_g, prev):
    width = heads * HEAD_DIM
    assert all(c % heads == 0 for c in (col_q, col_k, col_v, col_g))
    base, nb = grp.base, grp.nb
    l2d_rows = jnp.broadcast_to(l2d.astype(F32).reshape(2 * heads, 1), (2 * heads, HEAD_DIM))
    l2d_spec = pl.BlockSpec((2 * heads, HEAD_DIM), lambda b, j: (0, 0))
    table = pltpu.VMEM((heads, BLK, BLK), F32)

    def in_spec(col, reverse):
        def imap(b, j):
            jj = nb - 1 - j if reverse else j
            return (base + b * nb + jj, col // heads)
        return pl.BlockSpec((BLK, width), imap)

    y_bwd = pl.pallas_call(
        functools.partial(_ret_bwd_body, heads=heads, nb=nb, npad=grp.npad),
        grid=(grp.batch, nb),
        in_specs=[l2d_spec, in_spec(col_q, True), in_spec(col_k, True), in_spec(col_v, True)],
        out_specs=pl.BlockSpec((BLK, width), lambda b, j: (b * nb + nb - 1 - j, 0)),
        out_shape=jax.ShapeDtypeStruct((grp.batch * grp.rows, width), F32),
        scratch_shapes=[table, table, table],
        compiler_params=_cparams(("arbitrary", "arbitrary"), 32),
        name="retention_bwd",
    )(l2d_rows, proj, proj, proj)
    in_specs = [l2d_spec, in_spec(col_q, False), in_spec(col_k, False), in_spec(col_v, False),
                in_spec(col_g, False), pl.BlockSpec((BLK, width), lambda b, j: (b * nb + j, 0)),
                pl.BlockSpec((1, width), lambda b, j: (0, 0))]
    body = functools.partial(_ret_fwd_body, heads=heads, npad=grp.npad)
    args = [l2d_rows, proj, proj, proj, proj, y_bwd, gn_w.reshape(1, width).astype(F32)]
    body, in_specs, args, aliases = _carry_rows(body, in_specs, args, prev)
    return pl.pallas_call(
        body,
        grid=(grp.batch, nb),
        in_specs=in_specs,
        out_specs=pl.BlockSpec((BLK, width), lambda b, j: (base + b * nb + j, 0)),
        out_shape=jax.ShapeDtypeStruct((proj.shape[0], width), BF16),
        scratch_shapes=[table, table, table, table],
        input_output_aliases=aliases,
        compiler_params=_cparams(("arbitrary", "arbitrary"), 32),
        name="retention_fwd",
    )(*args)


def _softplus(x):
    return jnp.maximum(x, 0.0) + jnp.log1p(jnp.exp(-jnp.abs(x)))


def _roll_in_groups(x, shift):
    rows, lanes = x.shape
    grouped = x.reshape(rows // SUBLANES, SUBLANES, lanes)
    return pltpu.roll(grouped, shift, axis=1).reshape(rows, lanes)


def _lru_block(x_ref, prev_ref, next_ref, cw_ref, cb_ref, gw_ref, gb_ref, lam_ref, carry_ref, jj, *,
               nblocks, npad, seq_rows, reverse):
    width = x_ref.shape[-1]
    tail = prev_ref.shape[0]
    pi = jj * BLK - tail + lax.broadcasted_iota(jnp.int32, (tail, width), 0)
    prev = jnp.where(pi >= npad, prev_ref[...].astype(F32), 0.0)
    ni = (jj + 1) * BLK + lax.broadcasted_iota(jnp.int32, (tail, width), 0)
    nxt = jnp.where(ni < seq_rows, next_ref[...].astype(F32), 0.0)
    p1 = prev[tail - 1:tail, :]
    p2 = prev[tail - 2:tail - 1, :]
    n0 = nxt[0:1, :]
    rl = lax.broadcasted_iota(jnp.int32, (BLK, HEAD_DIM), 0)
    sub = lax.bitwise_and(rl, SUBLANES - 1)
    keeps = [(sh, (sub < SUBLANES - sh) if reverse else (sub >= sh)) for sh in (1, 2, 4)]
    ok = (jj * BLK + rl) >= npad
    row0, row1, row_last = rl == 0, rl == 1, rl == BLK - 1
    outs = []
    for c in range(nblocks):
        cs = slice(c * HEAD_DIM, (c + 1) * HEAD_DIM)
        x = jnp.where(ok, x_ref[:, cs].astype(F32), 0.0)
        x_m1 = jnp.where(row0, p1[:, cs], pltpu.roll(x, 1, axis=0))
        x_m2 = jnp.where(row0, p2[:, cs], jnp.where(row1, p1[:, cs], pltpu.roll(x, 2, axis=0)))
        x_p1 = jnp.where(row_last, n0[:, cs], pltpu.roll(x, BLK - 1, axis=0))
        xc = (cw_ref[0:1, cs] * x_m2 + cw_ref[1:2, cs] * x_m1 + cw_ref[2:3, cs] * x + cw_ref[3:4, cs] * x_p1) + cb_ref[:, cs]
        xcb = xc.astype(BF16)
        gr = jnp.dot(xcb, gw_ref[0, c], preferred_element_type=F32) + gb_ref[0:1, cs]
        gi = jnp.dot(xcb, gw_ref[1, c], preferred_element_type=F32) + gb_ref[1:2, cs]
        r = jax.nn.sigmoid(gr)
        i = jax.nn.sigmoid(gi)
        log_a = (-LRU_C * r) * _softplus(-lam_ref[:, cs])
        ea = jnp.exp(log_a)
        u = jnp.where(ok, jnp.sqrt(1.0 - ea * ea) * (i * xc), 0.0)
        a = jnp.where(ok, ea, 1.0)
        for sh, keep in keeps:
            shift = SUBLANES - sh if reverse else sh
            a_s = _roll_in_groups(a, shift)
            u_s = _roll_in_groups(u, shift)
            u = jnp.where(keep, a * u_s + u, u)
            a = jnp.where(keep, a * a_s, a)
        carry = carry_ref[:, cs]
        groups = [None] * (BLK // SUBLANES)
        edge = 0 if reverse else SUBLANES - 1
        for v in (reversed(range(len(groups))) if reverse else range(len(groups))):
            rows = slice(v * SUBLANES, (v + 1) * SUBLANES)
            hv = u[rows] + a[rows] * carry
            carry = hv[edge:edge + 1, :]
            groups[v] = hv
        carry_ref[:, cs] = carry
        outs.append(jnp.concatenate(groups, axis=0))
    return outs


def _lru_bwd_body(x_ref, prev_ref, next_ref, cw_ref, cb_ref, gw_ref, gb_ref, lam_ref, h_ref, carry_ref,
                  *, nb, nblocks, npad):
    j = pl.program_id(1)

    @pl.when(j == 0)
    def _():
        carry_ref[...] = jnp.zeros_like(carry_ref)

    outs = _lru_block(x_ref, prev_ref, next_ref, cw_ref, cb_ref, gw_ref, gb_ref, lam_ref, carry_ref, nb - 1 - j,
                      nblocks=nblocks, npad=npad, seq_rows=nb * BLK, reverse=True)
    for c, hcol in enumerate(outs):
        h_ref[:, c * HEAD_DIM:(c + 1) * HEAD_DIM] = hcol


def _lru_fwd_body(x_ref, prev_ref, next_ref, y_ref, hb_ref, cw_ref, cb_ref, gw_ref, gb_ref, lam_ref, o_ref, carry_ref,
                  *, nb, nblocks, npad):
    j = pl.program_id(1)

    @pl.when(j == 0)
    def _():
        carry_ref[...] = jnp.zeros_like(carry_ref)

    outs = _lru_block(x_ref, prev_ref, next_ref, cw_ref, cb_ref, gw_ref, gb_ref, lam_ref, carry_ref, j,
                      nblocks=nblocks, npad=npad, seq_rows=nb * BLK, reverse=False)
    for c, hcol in enumerate(outs):
        cs = slice(c * HEAD_DIM, (c + 1) * HEAD_DIM)
        gate = jax.nn.gelu(y_ref[:, cs].astype(F32))
        o_ref[:, cs] = ((hcol + hb_ref[:, cs]) * gate).astype(o_ref.dtype)


def _hawk(proj, conv_w, conv_b, gate_w, gate_b, lam, grp, total_rows, col_x, col_y, prev):
    nblocks = gate_w.shape[2]
    width = nblocks * HEAD_DIM
    tail = 16
    per = BLK // tail
    assert col_x % nblocks == 0 and col_y % nblocks == 0
    base, nb = grp.base, grp.nb
    last_tail = total_rows // tail - 1

    def specs(reverse):
        def blk(b, j):
            return base + b * nb + (nb - 1 - j if reverse else j)
        return [pl.BlockSpec((BLK, width), lambda b, j: (blk(b, j), col_x // nblocks)),
                pl.BlockSpec((tail, width), lambda b, j: (jnp.maximum(blk(b, j) * per - 1, 0), col_x // nblocks)),
                pl.BlockSpec((tail, width), lambda b, j: (jnp.minimum((blk(b, j) + 1) * per, last_tail), col_x // nblocks))]

    def param_specs(d):
        return [pl.BlockSpec((conv_w.shape[0], width), lambda b, j: (0, 0)),
                pl.BlockSpec((1, width), lambda b, j: (0, 0)),
                pl.BlockSpec((None, 2, nblocks, HEAD_DIM, HEAD_DIM), lambda b, j: (d, 0, 0, 0, 0)),
                pl.BlockSpec((None, 2, width), lambda b, j: (d, 0, 0)),
                pl.BlockSpec((None, 1, width), lambda b, j: (d, 0, 0))]

    params = (conv_w.astype(F32), conv_b.reshape(1, width).astype(F32), gate_w, gate_b.astype(F32),
              lam.reshape(2, 1, width).astype(F32))
    carry = pltpu.VMEM((1, width), F32)
    h_bwd = pl.pallas_call(
        functools.partial(_lru_bwd_body, nb=nb, nblocks=nblocks, npad=grp.npad),
        grid=(grp.batch, nb),
        in_specs=specs(True) + param_specs(1),
        out_specs=pl.BlockSpec((BLK, width), lambda b, j: (b * nb + nb - 1 - j, 0)),
        out_shape=jax.ShapeDtypeStruct((grp.batch * grp.rows, width), F32),
        scratch_shapes=[carry],
        compiler_params=_cparams(("arbitrary", "arbitrary"), 32),
        name="rglru_bwd",
    )(proj, proj, proj, *params)
    in_specs = (specs(False)
                + [pl.BlockSpec((BLK, width), lambda b, j: (base + b * nb + j, col_y // nblocks)),
                   pl.BlockSpec((BLK, width), lambda b, j: (b * nb + j, 0))]
                + param_specs(0))
    body = functools.partial(_lru_fwd_body, nb=nb, nblocks=nblocks, npad=grp.npad)
    body, in_specs, args, aliases = _carry_rows(body, in_specs, [proj, proj, proj, proj, h_bwd, *params], prev)
    return pl.pallas_call(
        body,
        grid=(grp.batch, nb),
        in_specs=in_specs,
        out_specs=pl.BlockSpec((BLK, width), lambda b, j: (base + b * nb + j, 0)),
        out_shape=jax.ShapeDtypeStruct((total_rows, width), BF16),
        scratch_shapes=[carry],
        input_output_aliases=aliases,
        compiler_params=_cparams(("arbitrary", "arbitrary"), 32),
        name="rglru_fwd",
    )(*args)


def _plan(batches, seqs):
    per_tile = MM_TILE // BLK
    nbs = [1 + s // BLK for s in seqs]
    total = sum(b * nb for b, nb in zip(batches, nbs))
    extra = (-total) % per_tile
    m0 = extra // batches[0] if extra % batches[0] == 0 else 0
    groups, base = [], 0
    for gi, (b, s, nb) in enumerate(zip(batches, seqs, nbs)):
        m = m0 if gi == 0 else 0
        groups.append(Group(base=base, batch=b, nb=nb + m, m=m, n_real=s))
        base += b * (nb + m)
    return groups, base * BLK


def kernel(x_prompt, x_sample, meta_tokens, norm_mix_w, w_in, attn_sink, qk_norm_q, qk_norm_k, ret_log2_decay,
           ret_gn_w, lru_conv_w, lru_conv_b, lru_gate_w, lru_gate_b, lru_lambda, w_branch, w_out, norm_mlp_w,
           w_up, w_down, final_norm_w):
    xs = (x_prompt, x_sample)
    d = x_prompt.shape[-1]
    depth = w_in.shape[0]
    bw = w_branch.shape[2]
    heads = bw // HEAD_DIM
    kv = heads // 4
    assert all(x.shape[1] % BLK == 0 and x.shape[1] % GRID_W == 0 for x in xs) and bw % HEAD_DIM == 0
    groups, total_rows = _plan([x.shape[0] for x in xs], [x.shape[1] for x in xs])

    h = None
    for grp, x in zip(groups, xs):
        h = _embed(x, meta_tokens, grp, total_rows, h)
    mask_parts = [np.tile(np.arange(grp.rows) >= grp.npad, grp.batch) for grp in groups]
    row_mask = jnp.asarray(np.concatenate(mask_parts).astype(np.float32).reshape(total_rows, 1))

    hw, kw = heads * HEAD_DIM, kv * HEAD_DIM
    names = ("aq", "ak", "av", "bq", "bk", "bv", "cq", "ck", "cv", "cg", "dx", "dy", "gate")
    widths = (hw, kw, kw, hw, kw, kw, bw, bw, bw, bw, bw, bw, N_BRANCH * d)
    assert sum(widths) == w_in.shape[-1]
    col = {name: int(off) // HEAD_DIM for name, off in zip(names, np.cumsum((0,) + widths))}
    w_in_b, w_branch_b, w_out_b, w_up_b, w_down_b = (w.astype(BF16) for w in (w_in, w_branch, w_out, w_up, w_down))
    gate_w_b = lru_gate_w.astype(BF16)
    rope_tables = [_rope_tables(grp) for grp in groups]

    def layer_fn(l, h):
        layer = jnp.full((1,), l, jnp.int32)
        at = lambda p: lax.dynamic_index_in_dim(p, l, axis=0, keepdims=False)
        xn = _rmsnorm(h, at(norm_mix_w), BF16)
        proj = _matmul(layer, xn, w_in_b)
        oa = ob = oc = od = None
        for grp, tables in zip(groups, rope_tables):
            oa = _attn_a(proj, at(attn_sink), grp, heads, kv, col["aq"], col["ak"], col["av"], oa)
            qb, kb, vb = _prep_b(proj, at(qk_norm_q), at(qk_norm_k), tables, grp, heads, kv,
                                 col["bq"], col["bk"], col["bv"])
            ob = _flash_b(qb, kb, vb, grp, heads, kv, total_rows, ob)
            oc = _retention(proj, at(ret_log2_decay), at(ret_gn_w), grp, heads,
                            col["cq"], col["ck"], col["cv"], col["cg"], oc)
            od = _hawk(proj, at(lru_conv_w), at(lru_conv_b), at(gate_w_b), at(lru_gate_b), at(lru_lambda),
                       grp, total_rows, col["dx"], col["dy"], od)
        merged = _merge(layer, row_mask, (oa, ob, oc, od), w_branch_b, proj, col["gate"] * HEAD_DIM)
        h, hw_b, ssq = _matmul(layer, merged, w_out_b, epilogue="residual", residual=h, out_dtype=F32,
                               next_norm_w=at(norm_mlp_w))
        hid = _matmul(layer, hw_b, w_up_b, epilogue="relu2", row_ssq=ssq)
        return _matmul(layer, hid, w_down_b, epilogue="residual", residual=h, out_dtype=F32)

    h = lax.fori_loop(0, depth, layer_fn, h)
    return tuple(_final_norm(h, final_norm_w, grp, x.dtype) for grp, x in zip(groups, xs))
```

```python
import functools
import math
from typing import NamedTuple

import numpy as np
import jax
import jax.numpy as jnp
from jax import lax
from jax.experimental import pallas as pl
from jax.experimental.pallas import tpu as pltpu

F32 = jnp.float32
BF16 = jnp.bfloat16

HEAD_DIM = 128
N_META = 16
BLK = 128
SUBLANES = 8
LANES = 128
PAD_ROWS = BLK - N_META
WINDOW = 128
GRID_W = 64
ROPE_THETA = 10000.0
ROPE_FREQS = HEAD_DIM // 4
CONV_LEFT = 2
LRU_C = 8.0
NORM_EPS = 1e-6
GN_EPS = 1e-5
NEG_INF = -1e30
N_BRANCH = 4

V7X_VMEM_BYTES = 64 * 1024 * 1024
MM_TILE = 1024
MM_TILE_K = 4096
MM_TALL_TILES = (1600, 2048)
FLASH_TK = 256
FLASH_UNROLL = 16


class Group(NamedTuple):
    base: int
    batch: int
    nb: int
    m: int
    n_real: int

    @property
    def npad(self):
        return self.m * BLK + PAD_ROWS

    @property
    def rows(self):
        return self.nb * BLK


def _cparams(sem, vmem_mib):
    assert vmem_mib * 1024 * 1024 < V7X_VMEM_BYTES
    return pltpu.CompilerParams(dimension_semantics=sem, vmem_limit_bytes=vmem_mib * 1024 * 1024)


def _largest_tile(n, cap):
    t = cap
    while n % t:
        t //= 2
    return t


def _carry_rows(body, in_specs, args, prev):
    if prev is None:
        return body, in_specs, args, {}
    n_in = len(in_specs)

    def body_without_prev(*refs):
        return body(*refs[:n_in], *refs[n_in + 1:])

    return body_without_prev, in_specs + [pl.BlockSpec(memory_space=pl.ANY)], args + [prev], {n_in: 0}


def _embed_body(x_ref, meta_ref, o_ref, *, m):
    j = pl.program_id(1)

    @pl.when(j > m)
    def _():
        o_ref[...] = x_ref[...].astype(o_ref.dtype)

    @pl.when(j == m)
    def _():
        o_ref[0:PAD_ROWS, :] = jnp.zeros((PAD_ROWS, o_ref.shape[1]), o_ref.dtype)
        o_ref[PAD_ROWS:BLK, :] = meta_ref[...].astype(o_ref.dtype)

    @pl.when(j < m)
    def _():
        o_ref[...] = jnp.zeros_like(o_ref)


def _embed(x, meta, grp, total_rows, prev):
    d = x.shape[-1]
    base, nb, m = grp.base, grp.nb, grp.m
    in_specs = [pl.BlockSpec((None, BLK, d), lambda b, j: (b, jnp.maximum(j - m - 1, 0), 0)),
                pl.BlockSpec((N_META, d), lambda b, j: (0, 0))]
    body, in_specs, args, aliases = _carry_rows(functools.partial(_embed_body, m=m), in_specs, [x, meta], prev)
    return pl.pallas_call(
        body,
        grid=(grp.batch, nb),
        in_specs=in_specs,
        out_specs=pl.BlockSpec((BLK, d), lambda b, j: (base + b * nb + j, 0)),
        out_shape=jax.ShapeDtypeStruct((total_rows, d), F32),
        input_output_aliases=aliases,
        compiler_params=_cparams(("parallel", "arbitrary"), 32),
        name="embed_tokens",
    )(*args)


def _rmsnorm_body(x_ref, w_ref, o_ref):
    x = x_ref[...].astype(F32)
    ms = jnp.mean(x * x, axis=-1, keepdims=True)
    o_ref[...] = ((x * lax.rsqrt(ms + NORM_EPS)) * w_ref[...]).astype(o_ref.dtype)


def _rmsnorm(h, w, out_dtype):
    t, d = h.shape
    tr = _largest_tile(t, 256)
    return pl.pallas_call(
        _rmsnorm_body,
        grid=(t // tr,),
        in_specs=[pl.BlockSpec((tr, d), lambda i: (i, 0)), pl.BlockSpec((1, d), lambda i: (0, 0))],
        out_specs=pl.BlockSpec((tr, d), lambda i: (i, 0)),
        out_shape=jax.ShapeDtypeStruct((t, d), out_dtype),
        compiler_params=_cparams(("parallel",), 32),
        name="rmsnorm",
    )(h, w.reshape(1, d).astype(F32))


def _final_norm(h, w, grp, out_dtype):
    _, d = h.shape
    nrb = grp.n_real // BLK
    return pl.pallas_call(
        _rmsnorm_body,
        grid=(grp.batch, nrb),
        in_specs=[pl.BlockSpec((BLK, d), lambda b, j: (grp.base + b * grp.nb + grp.m + 1 + j, 0)),
                  pl.BlockSpec((1, d), lambda b, j: (0, 0))],
        out_specs=pl.BlockSpec((None, BLK, d), lambda b, j: (b, j, 0)),
        out_shape=jax.ShapeDtypeStruct((grp.batch, grp.n_real, d), out_dtype),
        compiler_params=_cparams(("parallel", "parallel"), 32),
        name="final_norm",
    )(h, w.reshape(1, d).astype(F32))


def _mm_body(l_ref, a_ref, b_ref, *rest, epilogue, nk, row_norm, emit_normed):
    del l_ref
    rest = list(rest)
    ssq_ref = rest.pop(0) if row_norm else None
    r_ref = rest.pop(0) if epilogue == "residual" else None
    nw_ref = rest.pop(0) if emit_normed else None
    o_ref = rest.pop(0)
    hb_ref, sso_ref = (rest.pop(0), rest.pop(0)) if emit_normed else (None, None)
    scratch = rest

    def finish(acc):
        if row_norm:
            ms = jnp.sum(ssq_ref[...], axis=-1, keepdims=True) * (1.0 / (a_ref.shape[1] * nk))
            acc = acc * lax.rsqrt(ms + NORM_EPS)
        if epilogue == "relu2":
            acc = jnp.square(jnp.maximum(acc, 0.0))
        elif epilogue == "residual":
            acc = r_ref[...] + acc
        o_ref[...] = acc.astype(o_ref.dtype)
        if emit_normed:
            hb_ref[...] = (acc * nw_ref[...]).astype(hb_ref.dtype)
            sq = acc * acc
            sso_ref[...] = functools.reduce(
                jnp.add, [sq[:, t * LANES:(t + 1) * LANES] for t in range(sq.shape[1] // LANES)])

    def product():
        return jnp.dot(a_ref[...], b_ref[...], preferred_element_type=F32)

    if nk == 1:
        finish(product())
    elif epilogue == "residual" and o_ref.dtype == F32:
        k = pl.program_id(2)

        @pl.when(k == 0)
        def _():
            o_ref[...] = r_ref[...] + product()

        @pl.when(k > 0)
        def _():
            o_ref[...] += product()
    else:
        acc_ref, = scratch
        k = pl.program_id(2)
        prod = product()

        @pl.when(k == 0)
        def _():
            acc_ref[...] = prod

        @pl.when(k > 0)
        def _():
            acc_ref[...] += prod

        @pl.when(k == nk - 1)
        def _():
            finish(acc_ref[...])


def _matmul(layer, a, w, *, epilogue="none", residual=None, out_dtype=BF16, row_ssq=None, next_norm_w=None):
    t, kdim = a.shape
    n = w.shape[-1]
    tm = _largest_tile(t, MM_TILE)
    tn = _largest_tile(n, MM_TILE)
    tk = _largest_tile(kdim, MM_TILE_K)
    nk = kdim // tk
    row_norm, emit_normed = row_ssq is not None, next_norm_w is not None
    out_accumulates = epilogue == "residual" and out_dtype == F32
    assert not emit_normed or (nk == 1 and epilogue == "residual")
    a_mode = None
    if nk == 1 and epilogue != "residual":
        tall = [c for c in MM_TALL_TILES if t % c == 0]
        if tall:
            tm, a_mode = tall[0], pl.Buffered(1)
    in_specs = [pl.BlockSpec((tm, tk), lambda i, j, k, l: (i, k), pipeline_mode=a_mode),
                pl.BlockSpec((None, tk, tn), lambda i, j, k, l: (l[0], k, j))]
    args = [a, w]
    aliases = {}
    if row_norm:
        in_specs.append(pl.BlockSpec((tm, row_ssq.shape[1]), lambda i, j, k, l: (i, 0)))
        args.append(row_ssq)
    if epilogue == "residual":
        in_specs.append(pl.BlockSpec((tm, tn), lambda i, j, k, l: (i, j)))
        args.append(residual)
        aliases = {len(args): 0}
    out_specs = [pl.BlockSpec((tm, tn), lambda i, j, k, l: (i, j))]
    out_shape = [jax.ShapeDtypeStruct((t, n), out_dtype)]
    if emit_normed:
        in_specs.append(pl.BlockSpec((1, tn), lambda i, j, k, l: (0, j)))
        args.append(next_norm_w.reshape(1, n).astype(F32))
        out_specs += [pl.BlockSpec((tm, tn), lambda i, j, k, l: (i, j)),
                      pl.BlockSpec((tm, LANES), lambda i, j, k, l: (i, j))]
        out_shape += [jax.ShapeDtypeStruct((t, n), BF16), jax.ShapeDtypeStruct((t, n // tn * LANES), F32)]
    outs = pl.pallas_call(
        functools.partial(_mm_body, epilogue=epilogue, nk=nk, row_norm=row_norm, emit_normed=emit_normed),
        grid_spec=pltpu.PrefetchScalarGridSpec(
            num_scalar_prefetch=1,
            grid=(t // tm, n // tn, nk),
            in_specs=in_specs,
            out_specs=out_specs,
            scratch_shapes=[pltpu.VMEM((tm, tn), F32)] if nk > 1 and not out_accumulates else [],
        ),
        out_shape=out_shape,
        input_output_aliases=aliases,
        compiler_params=_cparams(("parallel", "parallel", "arbitrary"), 62),
        name="matmul_" + epilogue,
    )(layer, *args)
    return outs if emit_normed else outs[0]


def _merge_body(l_ref, mask_ref, oa_ref, ob_ref, oc_ref, od_ref, wb_ref, ga_ref, gb_ref, gc_ref, gd_ref, out_ref):
    del l_ref
    acc = None
    for b, (o_ref, g_ref) in enumerate(((oa_ref, ga_ref), (ob_ref, gb_ref), (oc_ref, gc_ref), (od_ref, gd_ref))):
        proj = jnp.dot(o_ref[...], wb_ref[b], preferred_element_type=F32)
        term = proj * jax.nn.sigmoid(g_ref[...].astype(F32))
        acc = term if acc is None else acc + term
    out_ref[...] = jnp.where(mask_ref[...] > 0.0, acc, 0.0).astype(out_ref.dtype)


def _merge(layer, row_mask, outs, w_branch, proj, gate_col0):
    t, bw = outs[0].shape
    d = w_branch.shape[-1]
    tm = _largest_tile(t, MM_TILE)
    tn = _largest_tile(d, 512)
    assert gate_col0 % tn == 0 and d % tn == 0
    o_spec = pl.BlockSpec((tm, bw), lambda i, j, l: (i, 0))

    def gate_spec(b):
        return pl.BlockSpec((tm, tn), lambda i, j, l: (i, (gate_col0 + b * d) // tn + j))

    return pl.pallas_call(
        _merge_body,
        grid_spec=pltpu.PrefetchScalarGridSpec(
            num_scalar_prefetch=1,
            grid=(t // tm, d // tn),
            in_specs=[pl.BlockSpec((tm, 1), lambda i, j, l: (i, 0)), o_spec, o_spec, o_spec, o_spec,
                      pl.BlockSpec((None, N_BRANCH, bw, tn), lambda i, j, l: (l[0], 0, 0, j)),
                      gate_spec(0), gate_spec(1), gate_spec(2), gate_spec(3)],
            out_specs=pl.BlockSpec((tm, tn), lambda i, j, l: (i, j)),
        ),
        out_shape=jax.ShapeDtypeStruct((t, d), BF16),
        compiler_params=_cparams(("parallel", "arbitrary"), 56),
        name="branch_merge",
    )(layer, row_mask, *outs, w_branch, proj, proj, proj, proj)


def _attn_a_body(sink_ref, q_ref, km_ref, kp_ref, kc_ref, kn_ref, vm_ref, vp_ref, vc_ref, vn_ref, o_ref,
                 *, m, nb, heads, kv):
    j = pl.program_id(1)
    g = heads // kv
    scale = HEAD_DIM ** -0.5
    shape = (BLK, 4 * BLK)
    ri = lax.broadcasted_iota(jnp.int32, shape, 0)
    ci = lax.broadcasted_iota(jnp.int32, shape, 1)
    slot = lax.shift_right_logical(ci, 7)
    kin = lax.bitwise_and(ci, BLK - 1)
    dist = jnp.abs(ri - kin + (2 - slot) * BLK)
    kblk = j + slot - 2
    band_ok = (slot >= 1) & (kblk >= m + 1) & (kblk <= nb - 1) & (dist <= WINDOW)
    meta_ok = (slot == 0) & (kin >= PAD_ROWS)
    valid = band_ok | meta_ok
    distf = jnp.where(slot >= 1, dist, 0).astype(F32)
    for kh in range(kv):
        cs = slice(kh * HEAD_DIM, (kh + 1) * HEAD_DIM)
        kcat = jnp.concatenate([km_ref[:, cs], kp_ref[:, cs], kc_ref[:, cs], kn_ref[:, cs]], axis=0)
        vcat = jnp.concatenate([vm_ref[:, cs], vp_ref[:, cs], vc_ref[:, cs], vn_ref[:, cs]], axis=0)
        hss = [slice((kh * g + gi) * HEAD_DIM, (kh * g + gi + 1) * HEAD_DIM) for gi in range(g)]
        q4 = jnp.concatenate([q_ref[:, hs] for hs in hss], axis=0)
        s4 = lax.dot_general(q4, kcat, (((1,), (1,)), ((), ())), preferred_element_type=F32)
        ps = []
        for gi in range(g):
            h = kh * g + gi
            slope = 2.0 ** (-8.0 * (h + 1) / heads)
            s = jnp.where(valid, s4[gi * BLK:(gi + 1) * BLK] * scale - slope * distf, NEG_INF)
            sink = sink_ref[h]
            mx = jnp.maximum(jnp.max(s, axis=-1, keepdims=True), sink)
            p = jnp.exp(s - mx)
            den = jnp.sum(p, axis=-1, keepdims=True) + jnp.exp(sink - mx)
            ps.append((p / den).astype(BF16))
        o4 = jnp.dot(jnp.concatenate(ps, axis=0), vcat, preferred_element_type=F32)
        for gi, hs in enumerate(hss):
            o_ref[:, hs] = o4[gi * BLK:(gi + 1) * BLK].astype(o_ref.dtype)


def _attn_a(proj, sink, grp, heads, kv, col_q, col_k, col_v, prev):
    qw, kw = heads * HEAD_DIM, kv * HEAD_DIM
    assert col_q % heads == 0 and col_k % kv == 0 and col_v % kv == 0
    base, nb, m = grp.base, grp.nb, grp.m

    def kv_spec(col, which):
        def imap(b, j):
            if which == "meta":
                jj = m
            elif which == "prev":
                jj = jnp.maximum(j - 1, 0)
            elif which == "next":
                jj = jnp.minimum(j + 1, nb - 1)
            else:
                jj = j
            return (base + b * nb + jj, col // kv)
        return pl.BlockSpec((BLK, kw), imap)

    order = ("meta", "prev", "own", "next")
    in_specs = ([pl.BlockSpec(memory_space=pltpu.SMEM),
                 pl.BlockSpec((BLK, qw), lambda b, j: (base + b * nb + j, col_q // heads))]
                + [kv_spec(col_k, w) for w in order] + [kv_spec(col_v, w) for w in order])
    body = functools.partial(_attn_a_body, m=m, nb=nb, heads=heads, kv=kv)
    body, in_specs, args, aliases = _carry_rows(body, in_specs, [sink.astype(F32)] + [proj] * 9, prev)
    return pl.pallas_call(
        body,
        grid=(grp.batch, nb),
        in_specs=in_specs,
        out_specs=pl.BlockSpec((BLK, qw), lambda b, j: (base + b * nb + j, 0)),
        out_shape=jax.ShapeDtypeStruct((proj.shape[0], qw), BF16),
        input_output_aliases=aliases,
        compiler_params=_cparams(("parallel", "parallel"), 32),
        name="attn_window",
    )(*args)


def _rope_tables(grp):
    r = np.arange(grp.rows)
    t = r - grp.npad - N_META
    row = np.where(t >= 0, t // GRID_W, -1)
    col = np.where(t >= 0, t % GRID_W, np.clip(r - grp.npad, 0, N_META - 1))
    pos = jnp.asarray(np.stack([row, col], axis=-1), F32)
    inv_freq = ROPE_THETA ** (-jnp.arange(ROPE_FREQS, dtype=F32) / ROPE_FREQS)
    ang = pos[:, :, None] * inv_freq
    cos, sin = jnp.cos(ang), jnp.sin(ang)
    cos_t = jnp.concatenate([cos[:, 0], cos[:, 0], cos[:, 1], cos[:, 1]], axis=-1)
    sin_t = jnp.concatenate([-sin[:, 0], sin[:, 0], -sin[:, 1], sin[:, 1]], axis=-1)
    return cos_t, sin_t


def _prep_b_body(qa_ref, qb_ref, k_ref, v_ref, cos_ref, sin_ref, gq_ref, gk_ref, qo_ref, ko_ref, vo_ref, *, heads, kv):
    cos, sin = cos_ref[...], sin_ref[...]
    lane = lax.broadcasted_iota(jnp.int32, (BLK, HEAD_DIM), 1)
    first_half = lax.bitwise_and(lane, 2 * ROPE_FREQS - 1) < ROPE_FREQS

    def norm_rope(x, w):
        x = x.astype(F32)
        ms = jnp.mean(x * x, axis=-1, keepdims=True)
        xn = (x * lax.rsqrt(ms + NORM_EPS)) * w
        partner = jnp.where(first_half, pltpu.roll(xn, HEAD_DIM - ROPE_FREQS, axis=1), pltpu.roll(xn, ROPE_FREQS, axis=1))
        return xn * cos + partner * sin

    half = heads // 2
    for h in range(heads):
        src = qa_ref if h < half else qb_ref
        hl = h % half
        qo_ref[:, h * HEAD_DIM:(h + 1) * HEAD_DIM] = norm_rope(
            src[:, hl * HEAD_DIM:(hl + 1) * HEAD_DIM], gq_ref[...]).astype(qo_ref.dtype)
    for h in range(kv):
        hs = slice(h * HEAD_DIM, (h + 1) * HEAD_DIM)
        ko_ref[h] = norm_rope(k_ref[:, hs], gk_ref[...]).T.astype(ko_ref.dtype)
    for h in range(kv):
        vo_ref[:, 2 * h * HEAD_DIM:(2 * h + 1) * HEAD_DIM] = v_ref[:, h * HEAD_DIM:(h + 1) * HEAD_DIM]
        vo_ref[:, (2 * h + 1) * HEAD_DIM:(2 * h + 2) * HEAD_DIM] = jnp.ones((BLK, HEAD_DIM), vo_ref.dtype)


def _prep_b(proj, gq, gk, tables, grp, heads, kv, col_q, col_k, col_v):
    qw, kw = heads * HEAD_DIM, kv * HEAD_DIM
    half = heads // 2
    assert heads % 2 == 0 and col_q % half == 0 and col_k % kv == 0 and col_v % kv == 0
    base, nb = grp.base, grp.nb
    cos_t, sin_t = tables
    tab_spec = pl.BlockSpec((BLK, HEAD_DIM), lambda b, j: (j, 0))
    vec_spec = pl.BlockSpec((1, HEAD_DIM), lambda b, j: (0, 0))

    def in_spec(width, col, per):
        return pl.BlockSpec((BLK, width), lambda b, j: (base + b * nb + j, col // per))

    def out_spec(width):
        return pl.BlockSpec((None, BLK, width), lambda b, j: (b, j, 0))

    return pl.pallas_call(
        functools.partial(_prep_b_body, heads=heads, kv=kv),
        grid=(grp.batch, nb),
        in_specs=[in_spec(qw // 2, col_q, half), in_spec(qw // 2, col_q + half, half),
                  in_spec(kw, col_k, kv), in_spec(kw, col_v, kv), tab_spec, tab_spec, vec_spec, vec_spec],
        out_specs=[out_spec(qw),
                   pl.BlockSpec((None, kv, None, HEAD_DIM, BLK), lambda b, j: (b, 0, j, 0, 0)),
                   out_spec(2 * kw)],
        out_shape=[jax.ShapeDtypeStruct((grp.batch, grp.rows, qw), BF16),
                   jax.ShapeDtypeStruct((grp.batch, kv, nb, HEAD_DIM, BLK), BF16),
                   jax.ShapeDtypeStruct((grp.batch, grp.rows, 2 * kw), BF16)],
        compiler_params=_cparams(("parallel", "parallel"), 32),
        name="attn_axial_prep",
    )(proj, proj, proj, proj, cos_t, sin_t, gq.reshape(1, HEAD_DIM).astype(F32), gk.reshape(1, HEAD_DIM).astype(F32))


def _flash_b_body(q_ref, k_ref, v_ref, o_ref, s_ref, m_ref, acc_ref, *, m, n_real, kb, g):
    rows, tk = g * BLK, kb * BLK
    exp2_scale = HEAD_DIM ** -0.5 * math.log2(math.e)
    q = jnp.concatenate([q_ref[:, i * HEAD_DIM:(i + 1) * HEAD_DIM] for i in range(g)], axis=0)
    tiles = lambda x: [x[:, t * BLK:(t + 1) * BLK] for t in range(x.shape[1] // BLK)]

    meta_mask = lax.broadcasted_iota(jnp.int32, (rows, BLK), 1) >= PAD_ROWS
    s_meta = jnp.where(meta_mask, jnp.dot(q, k_ref[m], preferred_element_type=F32), NEG_INF)
    m_ref[...] = s_meta

    def phase1(c, carry):
        first = m + 1 + c * kb
        kc = jnp.concatenate([k_ref[first + t] for t in range(kb)], axis=1)
        s = jnp.dot(q, kc, preferred_element_type=F32)
        s_ref[c] = s
        m_ref[...] = jnp.maximum(m_ref[...], functools.reduce(jnp.maximum, tiles(s)))
        return carry

    unroll = max(1, min(FLASH_UNROLL, n_real // tk // 2))
    lax.fori_loop(0, n_real // tk, phase1, 0, unroll=unroll)
    mx = jnp.broadcast_to(jnp.max(m_ref[...], axis=-1, keepdims=True), (rows, BLK))
    m_ref[...] = mx
    p_meta = jnp.exp2((s_meta - mx) * exp2_scale)
    acc_ref[...] = jnp.dot(p_meta.astype(BF16), v_ref[m * BLK:(m + 1) * BLK, :], preferred_element_type=F32)

    def phase2(c, carry):
        mxc = m_ref[...]
        p = jnp.concatenate([jnp.exp2((t - mxc) * exp2_scale).astype(BF16) for t in tiles(s_ref[c])], axis=1)
        start = pl.multiple_of((m + 1 + c * kb) * BLK, BLK)
        acc_ref[...] += jnp.dot(p, v_ref[pl.ds(start, tk), :], preferred_element_type=F32)
        return carry

    lax.fori_loop(0, n_real // tk, phase2, 0, unroll=unroll)
    acc = acc_ref[...]
    out = acc[:, :HEAD_DIM] / acc[:, HEAD_DIM:]
    o_ref[...] = jnp.concatenate([out[i * BLK:(i + 1) * BLK] for i in range(g)], axis=1).astype(o_ref.dtype)


def _flash_b(q, k, v, grp, heads, kv, total_rows, prev):
    g = heads // kv
    kb = FLASH_TK // BLK
    assert grp.n_real % FLASH_TK == 0
    rows = g * BLK
    base, nb = grp.base, grp.nb
    in_specs = [pl.BlockSpec((None, BLK, g * HEAD_DIM), lambda b, kh, j: (b, j, kh)),
                pl.BlockSpec((None, None, nb, HEAD_DIM, BLK), lambda b, kh, j: (b, kh, 0, 0, 0)),
                pl.BlockSpec((None, grp.rows, 2 * HEAD_DIM), lambda b, kh, j: (b, 0, kh))]
    body = functools.partial(_flash_b_body, m=grp.m, n_real=grp.n_real, kb=kb, g=g)
    body, in_specs, args, aliases = _carry_rows(body, in_specs, [q, k, v], prev)
    return pl.pallas_call(
        body,
        grid=(grp.batch, kv, nb),
        in_specs=in_specs,
        out_specs=pl.BlockSpec((BLK, g * HEAD_DIM), lambda b, kh, j: (base + b * nb + j, kh)),
        out_shape=jax.ShapeDtypeStruct((total_rows, heads * HEAD_DIM), BF16),
        scratch_shapes=[pltpu.VMEM((grp.n_real // FLASH_TK, rows, FLASH_TK), F32),
                        pltpu.VMEM((rows, BLK), F32), pltpu.VMEM((rows, 2 * HEAD_DIM), F32)],
        input_output_aliases=aliases,
        compiler_params=_cparams(("parallel", "parallel", "arbitrary"), 48),
        name="attn_axial",
    )(*args)


def _ret_tables(l2d_ref, heads, tab_in, tab_q, tab_k, *, backward_only):
    ri = lax.broadcasted_iota(jnp.int32, (BLK, BLK), 0).astype(F32)
    ci = lax.broadcasted_iota(jnp.int32, (BLK, BLK), 1).astype(F32)
    for h in range(heads):
        lg_b = jnp.log1p(-jnp.exp2(l2d_ref[heads + h:heads + h + 1, :]))
        if backward_only:
            tab_q[h] = jnp.exp(lg_b * (BLK - ri))
            tab_k[h] = jnp.exp(lg_b * ri)
        else:
            lg_f = jnp.log1p(-jnp.exp2(l2d_ref[h:h + 1, :]))
            diff = ri - ci
            fwd = jnp.exp(lg_f * jnp.maximum(diff, 0.0))
            bwd = jnp.exp(lg_b * jnp.maximum(-diff, 0.0))
            tab_in[h] = jnp.where(diff >= 0, fwd, bwd)
            tab_q[h] = jnp.exp(lg_f * (ri + 1.0))
            tab_k[h] = jnp.exp(lg_f * (BLK - 1.0 - ri))


def _ret_bwd_body(l2d_ref, q_ref, k_ref, v_ref, y_ref, state, tab_q, tab_k, *, heads, nb, npad):
    b, j = pl.program_id(0), pl.program_id(1)
    jj = nb - 1 - j
    scale = HEAD_DIM ** -0.5

    @pl.when((b == 0) & (j == 0))
    def _():
        _ret_tables(l2d_ref, heads, None, tab_q, tab_k, backward_only=True)

    @pl.when(j == 0)
    def _():
        state[...] = jnp.zeros_like(state)

    valid = (jj * BLK + lax.broadcasted_iota(jnp.int32, (BLK, HEAD_DIM), 0)) >= npad
    for h in range(heads):
        hs = slice(h * HEAD_DIM, (h + 1) * HEAD_DIM)
        ks = jnp.where(valid, k_ref[:, hs].astype(F32) * scale, 0.0)
        vh = jnp.where(valid, v_ref[:, hs], jnp.zeros((), v_ref.dtype))
        st = state[h]
        y_ref[:, hs] = jnp.dot(q_ref[:, hs], st.astype(BF16), preferred_element_type=F32) * tab_q[h]
        kd = (ks * tab_k[h]).astype(BF16)
        upd = lax.dot_general(kd, vh, (((0,), (0,)), ((), ())), preferred_element_type=F32)
        state[h] = st * tab_q[h][0:1, :] + upd


def _ret_fwd_body(l2d_ref, q_ref, k_ref, v_ref, g_ref, yb_ref, gn_ref, o_ref, state, tab_in, tab_q, tab_k,
                  *, heads, npad):
    b, j = pl.program_id(0), pl.program_id(1)
    scale = HEAD_DIM ** -0.5

    @pl.when((b == 0) & (j == 0))
    def _():
        _ret_tables(l2d_ref, heads, tab_in, tab_q, tab_k, backward_only=False)

    @pl.when(j == 0)
    def _():
        state[...] = jnp.zeros_like(state)

    valid = (j * BLK + lax.broadcasted_iota(jnp.int32, (BLK, HEAD_DIM), 0)) >= npad
    for h in range(heads):
        hs = slice(h * HEAD_DIM, (h + 1) * HEAD_DIM)
        qh = q_ref[:, hs]
        ks = jnp.where(valid, k_ref[:, hs].astype(F32) * scale, 0.0)
        vh = jnp.where(valid, v_ref[:, hs], jnp.zeros((), v_ref.dtype))
        st = state[h]
        scores = lax.dot_general(qh, ks.astype(BF16), (((1,), (1,)), ((), ())), preferred_element_type=F32) * tab_in[h]
        inner = jnp.dot(scores.astype(BF16), vh, preferred_element_type=F32)
        cross = jnp.dot(qh, st.astype(BF16), preferred_element_type=F32) * tab_q[h]
        kd = (ks * tab_k[h]).astype(BF16)
        upd = lax.dot_general(kd, vh, (((0,), (0,)), ((), ())), preferred_element_type=F32)
        state[h] = st * tab_q[h][BLK - 1:BLK, :] + upd
        y = inner + cross + yb_ref[:, hs]
        mu = jnp.mean(y, axis=-1, keepdims=True)
        yc = y - mu
        var = jnp.mean(yc * yc, axis=-1, keepdims=True)
        yn = (yc * lax.rsqrt(var + GN_EPS)) * gn_ref[:, hs]
        o_ref[:, hs] = (jax.nn.silu(g_ref[:, hs].astype(F32)) * yn).astype(o_ref.dtype)


def _retention(proj, l2d, gn_w, grp, heads, col_q, col_k, col_v, col_g, prev):
    width = heads * HEAD_DIM
    assert all(c % heads == 0 for c in (col_q, col_k, col_v, col_g))
    base, nb = grp.base, grp.nb
    l2d_rows = jnp.broadcast_to(l2d.astype(F32).reshape(2 * heads, 1), (2 * heads, HEAD_DIM))
    l2d_spec = pl.BlockSpec((2 * heads, HEAD_DIM), lambda b, j: (0, 0))
    table = pltpu.VMEM((heads, BLK, BLK), F32)

    def in_spec(col, reverse):
        def imap(b, j):
            jj = nb - 1 - j if reverse else j
            return (base + b * nb + jj, col // heads)
        return pl.BlockSpec((BLK, width), imap)

    y_bwd = pl.pallas_call(
        functools.partial(_ret_bwd_body, heads=heads, nb=nb, npad=grp.npad),
        grid=(grp.batch, nb),
        in_specs=[l2d_spec, in_spec(col_q, True), in_spec(col_k, True), in_spec(col_v, True)],
        out_specs=pl.BlockSpec((BLK, width), lambda b, j: (b * nb + nb - 1 - j, 0)),
        out_shape=jax.ShapeDtypeStruct((grp.batch * grp.rows, width), F32),
        scratch_shapes=[table, table, table],
        compiler_params=_cparams(("arbitrary", "arbitrary"), 32),
        name="retention_bwd",
    )(l2d_rows, proj, proj, proj)
    in_specs = [l2d_spec, in_spec(col_q, False), in_spec(col_k, False), in_spec(col_v, False),
                in_spec(col_g, False), pl.BlockSpec((BLK, width), lambda b, j: (b * nb + j, 0)),
                pl.BlockSpec((1, width), lambda b, j: (0, 0))]
    body = functools.partial(_ret_fwd_body, heads=heads, npad=grp.npad)
    args = [l2d_rows, proj, proj, proj, proj, y_bwd, gn_w.reshape(1, width).astype(F32)]
    body, in_specs, args, aliases = _carry_rows(body, in_specs, args, prev)
    return pl.pallas_call(
        body,
        grid=(grp.batch, nb),
        in_specs=in_specs,
        out_specs=pl.BlockSpec((BLK, width), lambda b, j: (base + b * nb + j, 0)),
        out_shape=jax.ShapeDtypeStruct((proj.shape[0], width), BF16),
        scratch_shapes=[table, table, table, table],
        input_output_aliases=aliases,
        compiler_params=_cparams(("arbitrary", "arbitrary"), 32),
        name="retention_fwd",
    )(*args)


def _softplus(x):
    return jnp.maximum(x, 0.0) + jnp.log1p(jnp.exp(-jnp.abs(x)))


def _roll_in_groups(x, shift):
    rows, lanes = x.shape
    grouped = x.reshape(rows // SUBLANES, SUBLANES, lanes)
    return pltpu.roll(grouped, shift, axis=1).reshape(rows, lanes)


def _lru_block(x_ref, prev_ref, next_ref, cw_ref, cb_ref, gw_ref, gb_ref, lam_ref, carry_ref, jj, *,
               nblocks, npad, seq_rows, reverse):
    width = x_ref.shape[-1]
    tail = prev_ref.shape[0]
    pi = jj * BLK - tail + lax.broadcasted_iota(jnp.int32, (tail, width), 0)
    prev = jnp.where(pi >= npad, prev_ref[...].astype(F32), 0.0)
    ni = (jj + 1) * BLK + lax.broadcasted_iota(jnp.int32, (tail, width), 0)
    nxt = jnp.where(ni < seq_rows, next_ref[...].astype(F32), 0.0)
    p1 = prev[tail - 1:tail, :]
    p2 = prev[tail - 2:tail - 1, :]
    n0 = nxt[0:1, :]
    rl = lax.broadcasted_iota(jnp.int32, (BLK, HEAD_DIM), 0)
    sub = lax.bitwise_and(rl, SUBLANES - 1)
    keeps = [(sh, (sub < SUBLANES - sh) if reverse else (sub >= sh)) for sh in (1, 2, 4)]
    ok = (jj * BLK + rl) >= npad
    row0, row1, row_last = rl == 0, rl == 1, rl == BLK - 1
    outs = []
    for c in range(nblocks):
        cs = slice(c * HEAD_DIM, (c + 1) * HEAD_DIM)
        x = jnp.where(ok, x_ref[:, cs].astype(F32), 0.0)
        x_m1 = jnp.where(row0, p1[:, cs], pltpu.roll(x, 1, axis=0))
        x_m2 = jnp.where(row0, p2[:, cs], jnp.where(row1, p1[:, cs], pltpu.roll(x, 2, axis=0)))
        x_p1 = jnp.where(row_last, n0[:, cs], pltpu.roll(x, BLK - 1, axis=0))
        xc = (cw_ref[0:1, cs] * x_m2 + cw_ref[1:2, cs] * x_m1 + cw_ref[2:3, cs] * x + cw_ref[3:4, cs] * x_p1) + cb_ref[:, cs]
        xcb = xc.astype(BF16)
        gr = jnp.dot(xcb, gw_ref[0, c], preferred_element_type=F32) + gb_ref[0:1, cs]
        gi = jnp.dot(xcb, gw_ref[1, c], preferred_element_type=F32) + gb_ref[1:2, cs]
        r = jax.nn.sigmoid(gr)
        i = jax.nn.sigmoid(gi)
        log_a = (-LRU_C * r) * _softplus(-lam_ref[:, cs])
        ea = jnp.exp(log_a)
        u = jnp.where(ok, jnp.sqrt(1.0 - ea * ea) * (i * xc), 0.0)
        a = jnp.where(ok, ea, 1.0)
        for sh, keep in keeps:
            shift = SUBLANES - sh if reverse else sh
            a_s = _roll_in_groups(a, shift)
            u_s = _roll_in_groups(u, shift)
            u = jnp.where(keep, a * u_s + u, u)
            a = jnp.where(keep, a * a_s, a)
        carry = carry_ref[:, cs]
        groups = [None] * (BLK // SUBLANES)
        edge = 0 if reverse else SUBLANES - 1
        for v in (reversed(range(len(groups))) if reverse else range(len(groups))):
            rows = slice(v * SUBLANES, (v + 1) * SUBLANES)
            hv = u[rows] + a[rows] * carry
            carry = hv[edge:edge + 1, :]
            groups[v] = hv
        carry_ref[:, cs] = carry
        outs.append(jnp.concatenate(groups, axis=0))
    return outs


def _lru_bwd_body(x_ref, prev_ref, next_ref, cw_ref, cb_ref, gw_ref, gb_ref, lam_ref, h_ref, carry_ref,
                  *, nb, nblocks, npad):
    j = pl.program_id(1)

    @pl.when(j == 0)
    def _():
        carry_ref[...] = jnp.zeros_like(carry_ref)

    outs = _lru_block(x_ref, prev_ref, next_ref, cw_ref, cb_ref, gw_ref, gb_ref, lam_ref, carry_ref, nb - 1 - j,
                      nblocks=nblocks, npad=npad, seq_rows=nb * BLK, reverse=True)
    for c, hcol in enumerate(outs):
        h_ref[:, c * HEAD_DIM:(c + 1) * HEAD_DIM] = hcol


def _lru_fwd_body(x_ref, prev_ref, next_ref, y_ref, hb_ref, cw_ref, cb_ref, gw_ref, gb_ref, lam_ref, o_ref, carry_ref,
                  *, nb, nblocks, npad):
    j = pl.program_id(1)

    @pl.when(j == 0)
    def _():
        carry_ref[...] = jnp.zeros_like(carry_ref)

    outs = _lru_block(x_ref, prev_ref, next_ref, cw_ref, cb_ref, gw_ref, gb_ref, lam_ref, carry_ref, j,
                      nblocks=nblocks, npad=npad, seq_rows=nb * BLK, reverse=False)
    for c, hcol in enumerate(outs):
        cs = slice(c * HEAD_DIM, (c + 1) * HEAD_DIM)
        gate = jax.nn.gelu(y_ref[:, cs].astype(F32))
        o_ref[:, cs] = ((hcol + hb_ref[:, cs]) * gate).astype(o_ref.dtype)


def _hawk(proj, conv_w, conv_b, gate_w, gate_b, lam, grp, total_rows, col_x, col_y, prev):
    nblocks = gate_w.shape[2]
    width = nblocks * HEAD_DIM
    tail = 16
    per = BLK // tail
    assert col_x % nblocks == 0 and col_y % nblocks == 0
    base, nb = grp.base, grp.nb
    last_tail = total_rows // tail - 1

    def specs(reverse):
        def blk(b, j):
            return base + b * nb + (nb - 1 - j if reverse else j)
        return [pl.BlockSpec((BLK, width), lambda b, j: (blk(b, j), col_x // nblocks)),
                pl.BlockSpec((tail, width), lambda b, j: (jnp.maximum(blk(b, j) * per - 1, 0), col_x // nblocks)),
                pl.BlockSpec((tail, width), lambda b, j: (jnp.minimum((blk(b, j) + 1) * per, last_tail), col_x // nblocks))]

    def param_specs(d):
        return [pl.BlockSpec((conv_w.shape[0], width), lambda b, j: (0, 0)),
                pl.BlockSpec((1, width), lambda b, j: (0, 0)),
                pl.BlockSpec((None, 2, nblocks, HEAD_DIM, HEAD_DIM), lambda b, j: (d, 0, 0, 0, 0)),
                pl.BlockSpec((None, 2, width), lambda b, j: (d, 0, 0)),
                pl.BlockSpec((None, 1, width), lambda b, j: (d, 0, 0))]

    params = (conv_w.astype(F32), conv_b.reshape(1, width).astype(F32), gate_w, gate_b.astype(F32),
              lam.reshape(2, 1, width).astype(F32))
    carry = pltpu.VMEM((1, width), F32)
    h_bwd = pl.pallas_call(
        functools.partial(_lru_bwd_body, nb=nb, nblocks=nblocks, npad=grp.npad),
        grid=(grp.batch, nb),
        in_specs=specs(True) + param_specs(1),
        out_specs=pl.BlockSpec((BLK, width), lambda b, j: (b * nb + nb - 1 - j, 0)),
        out_shape=jax.ShapeDtypeStruct((grp.batch * grp.rows, width), F32),
        scratch_shapes=[carry],
        compiler_params=_cparams(("arbitrary", "arbitrary"), 32),
        name="rglru_bwd",
    )(proj, proj, proj, *params)
    in_specs = (specs(False)
                + [pl.BlockSpec((BLK, width), lambda b, j: (base + b * nb + j, col_y // nblocks)),
                   pl.BlockSpec((BLK, width), lambda b, j: (b * nb + j, 0))]
                + param_specs(0))
    body = functools.partial(_lru_fwd_body, nb=nb, nblocks=nblocks, npad=grp.npad)
    body, in_specs, args, aliases = _carry_rows(body, in_specs, [proj, proj, proj, proj, h_bwd, *params], prev)
    return pl.pallas_call(
        body,
        grid=(grp.batch, nb),
        in_specs=in_specs,
        out_specs=pl.BlockSpec((BLK, width), lambda b, j: (base + b * nb + j, 0)),
        out_shape=jax.ShapeDtypeStruct((total_rows, width), BF16),
        scratch_shapes=[carry],
        input_output_aliases=aliases,
        compiler_params=_cparams(("arbitrary", "arbitrary"), 32),
        name="rglru_fwd",
    )(*args)


def _plan(batches, seqs):
    per_tile = MM_TILE // BLK
    nbs = [1 + s // BLK for s in seqs]
    total = sum(b * nb for b, nb in zip(batches, nbs))
    extra = (-total) % per_tile
    m0 = extra // batches[0] if extra % batches[0] == 0 else 0
    groups, base = [], 0
    for gi, (b, s, nb) in enumerate(zip(batches, seqs, nbs)):
        m = m0 if gi == 0 else 0
        groups.append(Group(base=base, batch=b, nb=nb + m, m=m, n_real=s))
        base += b * (nb + m)
    return groups, base * BLK


def kernel(x_prompt, x_sample, meta_tokens, norm_mix_w, w_in, attn_sink, qk_norm_q, qk_norm_k, ret_log2_decay,
           ret_gn_w, lru_conv_w, lru_conv_b, lru_gate_w, lru_gate_b, lru_lambda, w_branch, w_out, norm_mlp_w,
           w_up, w_down, final_norm_w):
    xs = (x_prompt, x_sample)
    d = x_prompt.shape[-1]
    depth = w_in.shape[0]
    bw = w_branch.shape[2]
    heads = bw // HEAD_DIM
    kv = heads // 4
    assert all(x.shape[1] % BLK == 0 and x.shape[1] % GRID_W == 0 for x in xs) and bw % HEAD_DIM == 0
    groups, total_rows = _plan([x.shape[0] for x in xs], [x.shape[1] for x in xs])

    h = None
    for grp, x in zip(groups, xs):
        h = _embed(x, meta_tokens, grp, total_rows, h)
    mask_parts = [np.tile(np.arange(grp.rows) >= grp.npad, grp.batch) for grp in groups]
    row_mask = jnp.asarray(np.concatenate(mask_parts).astype(np.float32).reshape(total_rows, 1))

    hw, kw = heads * HEAD_DIM, kv * HEAD_DIM
    names = ("aq", "ak", "av", "bq", "bk", "bv", "cq", "ck", "cv", "cg", "dx", "dy", "gate")
    widths = (hw, kw, kw, hw, kw, kw, bw, bw, bw, bw, bw, bw, N_BRANCH * d)
    assert sum(widths) == w_in.shape[-1]
    col = {name: int(off) // HEAD_DIM for name, off in zip(names, np.cumsum((0,) + widths))}
    w_in_b, w_branch_b, w_out_b, w_up_b, w_down_b = (w.astype(BF16) for w in (w_in, w_branch, w_out, w_up, w_down))
    gate_w_b = lru_gate_w.astype(BF16)
    rope_tables = [_rope_tables(grp) for grp in groups]

    def layer_fn(l, h):
        layer = jnp.full((1,), l, jnp.int32)
        at = lambda p: lax.dynamic_index_in_dim(p, l, axis=0, keepdims=False)
        xn = _rmsnorm(h, at(norm_mix_w), BF16)
        proj = _matmul(layer, xn, w_in_b)
        oa = ob = oc = od = None
        for grp, tables in zip(groups, rope_tables):
            oa = _attn_a(proj, at(attn_sink), grp, heads, kv, col["aq"], col["ak"], col["av"], oa)
            qb, kb, vb = _prep_b(proj, at(qk_norm_q), at(qk_norm_k), tables, grp, heads, kv,
                                 col["bq"], col["bk"], col["bv"])
            ob = _flash_b(qb, kb, vb, grp, heads, kv, total_rows, ob)
            oc = _retention(proj, at(ret_log2_decay), at(ret_gn_w), grp, heads,
                            col["cq"], col["ck"], col["cv"], col["cg"], oc)
            od = _hawk(proj, at(lru_conv_w), at(lru_conv_b), at(gate_w_b), at(lru_gate_b), at(lru_lambda),
                       grp, total_rows, col["dx"], col["dy"], od)
        merged = _merge(layer, row_mask, (oa, ob, oc, od), w_branch_b, proj, col["gate"] * HEAD_DIM)
        h, hw_b, ssq = _matmul(layer, merged, w_out_b, epilogue="residual", residual=h, out_dtype=F32,
                               next_norm_w=at(norm_mlp_w))
        hid = _matmul(layer, hw_b, w_up_b, epilogue="relu2", row_ssq=ssq)
        return _matmul(layer, hid, w_down_b, epilogue="residual", residual=h, out_dtype=F32)

    h = lax.fori_loop(0, depth, layer_fn, h)
    return tuple(_final_norm(h, final_norm_w, grp, x.dtype) for grp, x in zip(groups, xs))
```
